```python
import math
import jax, jax.numpy as jnp
from jax import lax
import numpy as np

D_MODEL = 1024
BATCH = 16
SEQ = 2048
DEPTH = 4

CHUNK = 64
Q_BLOCK = 128
N_HEADS = 8
HEAD_DIM = 128
D_ATT = N_HEADS * HEAD_DIM
D_RNN = D_MODEL
N_RNN_BLOCKS = 8
RNN_BLOCK = D_RNN // N_RNN_BLOCKS
CONV_WIDTH = 4
RG_C = 8.0
D_FF = -(-8 * D_MODEL // (3 * 256)) * 256
D_PLE = 256
DN_ALPHA = float((2 * DEPTH) ** 0.25)
DN_BETA = float((8 * DEPTH) ** -0.25)
LN_EPS = 1e-5

COLS = [D_ATT, D_ATT, D_ATT, N_HEADS, D_RNN, D_RNN, D_MODEL, D_MODEL]
N_IN = sum(COLS)
SPLITS = list(np.cumsum(COLS)[:-1].tolist())

kernel_name = "hybrid_fox_rglru_deepnorm_encoder"


def layer_norm(x, g, b):
    xf = x.astype(jnp.float32)
    mu = jnp.mean(xf, axis=-1, keepdims=True)
    var = jnp.mean(jnp.square(xf - mu), axis=-1, keepdims=True)
    y = (xf - mu) * lax.rsqrt(var + LN_EPS)
    return (y * g.astype(jnp.float32) + b.astype(jnp.float32)).astype(x.dtype)


def forgetting_attention(q, k, v, logf):
    S = q.shape[1]
    scale = 1.0 / math.sqrt(HEAD_DIM)
    c = jnp.cumsum(logf, axis=1).transpose(0, 2, 1)
    outs = []
    for blk in range(S // Q_BLOCK):
        t0, t1 = blk * Q_BLOCK, (blk + 1) * Q_BLOCK
        qb, kb, vb = q[:, t0:t1], k[:, :t1], v[:, :t1]
        s = jnp.einsum('bqhd,bkhd->bhqk', qb, kb).astype(jnp.float32) * scale
        s = s + c[:, :, t0:t1, None] - c[:, :, None, :t1]
        q_pos = t0 + jnp.arange(Q_BLOCK)[:, None]
        k_pos = jnp.arange(t1)[None, :]
        s = jnp.where(k_pos <= q_pos, s, -jnp.inf)
        pr = jax.nn.softmax(s, axis=-1)
        outs.append(jnp.einsum('bhqk,bkhd->bqhd', pr.astype(vb.dtype), vb))
    return jnp.concatenate(outs, axis=1)


def causal_depthwise_conv(x, w, b):
    y = lax.conv_general_dilated(
        x, w[:, None, :].astype(x.dtype), window_strides=(1,),
        padding=[(CONV_WIDTH - 1, 0)],
        dimension_numbers=('NWC', 'WIO', 'NWC'),
        feature_group_count=x.shape[-1])
    return y + b


def block_diag_linear(x, w, b):
    B, S, _ = x.shape
    xr = x.reshape(B, S, N_RNN_BLOCKS, RNN_BLOCK)
    return jnp.einsum('bsnc,ncd->bsnd', xr, w).reshape(B, S, D_RNN) + b


def _lin_rec_combine(e1, e2):
    a1, b1 = e1
    a2, b2 = e2
    return a1 * a2, a2 * b1 + b2


def rg_lru_branch(rx, ry, conv_w, conv_b, w_a, b_a, w_x, b_x, lam):
    xc = causal_depthwise_conv(rx, conv_w, conv_b)
    r = jax.nn.sigmoid(block_diag_linear(xc, w_a, b_a).astype(jnp.float32))
    i = jax.nn.sigmoid(block_diag_linear(xc, w_x, b_x).astype(jnp.float32))
    log_a = -RG_C * jax.nn.softplus(-lam.astype(jnp.float32)) * r
    a = jnp.exp(log_a)
    mult = jnp.sqrt(-jnp.expm1(2.0 * log_a))
    u = mult * (i * xc.astype(jnp.float32))
    _, h = lax.associative_scan(_lin_rec_combine, (a, u), axis=1)
    return h.astype(rx.dtype) * jax.nn.gelu(ry)


def hybrid_mixer(u, w_in, b_forget, conv_w, conv_b, w_a, b_a, w_x, b_x, lam,
                 w_br_att, w_br_rnn, b_merge, w_out):
    B, S, _ = u.shape
    z = u @ w_in
    q, k, v, f_logit, rx, ry, ga, gb = jnp.split(z, SPLITS, axis=-1)
    q = q.reshape(B, S, N_HEADS, HEAD_DIM)
    k = k.reshape(B, S, N_HEADS, HEAD_DIM)
    v = v.reshape(B, S, N_HEADS, HEAD_DIM)
    logf = jax.nn.log_sigmoid((f_logit + b_forget).astype(jnp.float32))
    att = forgetting_attention(q, k, v, logf).reshape(B, S, D_ATT)
    rnn = rg_lru_branch(rx, ry, conv_w, conv_b, w_a, b_a, w_x, b_x, lam)
    ya = att @ w_br_att
    yb = rnn @ w_br_rnn
    merged = jax.nn.sigmoid(ga + b_merge[0]) * ya + jax.nn.sigmoid(gb + b_merge[1]) * yb
    return merged @ w_out


def swiglu(x, w_in, w_out):
    hg, hu = jnp.split(x @ w_in, 2, axis=-1)
    return (jax.nn.silu(hg) * hu) @ w_out


def setup_inputs(seed: int = 0) -> dict:
    key = jax.random.key(seed)
    ks = jax.random.split(key, 32)
    f32 = jnp.float32
    L, D = DEPTH, D_MODEL

    def nrm(k, shape, scale):
        return jax.random.normal(k, shape, f32) * scale

    u_lam = jax.random.uniform(ks[10], (L, D_RNN), f32, 0.9, 0.999)
    a0 = u_lam ** (1.0 / RG_C)
    rg_lambda = jnp.log(a0) - jnp.log1p(-a0)

    return {
        "x": nrm(ks[0], (BATCH, SEQ, D), 1.0),
        "p": nrm(ks[1], (DEPTH, BATCH, SEQ, D_PLE), 1.0),
        "ln_in_g": 1.0 + nrm(ks[2], (D,), 0.02),
        "ln_in_b": nrm(ks[3], (D,), 0.02),
        "w_in": nrm(ks[4], (L, D, N_IN), D ** -0.5),
        "b_forget": jax.random.uniform(ks[5], (L, N_HEADS), f32, 1.0, 6.0),
        "conv_w": nrm(ks[6], (L, CONV_WIDTH, D_RNN), CONV_WIDTH ** -0.5),
        "conv_b": nrm(ks[7], (L, D_RNN), 0.02),
        "rg_w_a": nrm(ks[8], (L, N_RNN_BLOCKS, RNN_BLOCK, RNN_BLOCK), RNN_BLOCK ** -0.5),
        "rg_b_a": nrm(ks[9], (L, D_RNN), 0.02),
        "rg_w_x": nrm(ks[11], (L, N_RNN_BLOCKS, RNN_BLOCK, RNN_BLOCK), RNN_BLOCK ** -0.5),
        "rg_b_x": nrm(ks[12], (L, D_RNN), 0.02),
        "rg_lambda": rg_lambda,
        "w_branch_att": nrm(ks[13], (L, D_ATT, D), D_ATT ** -0.5),
        "w_branch_rnn": nrm(ks[14], (L, D_RNN, D), D_RNN ** -0.5),
        "b_merge": nrm(ks[15], (L, 2, D), 0.02),
        "w_out": nrm(ks[16], (L, D, D), D ** -0.5 * DN_BETA),
        "ln_mix_g": 1.0 + nrm(ks[17], (L, D), 0.02),
        "ln_mix_b": nrm(ks[18], (L, D), 0.02),
        "w_ffn_in": nrm(ks[19], (L, D, 2 * D_FF), D ** -0.5),
        "w_ffn_out": nrm(ks[20], (L, D_FF, D), D_FF ** -0.5 * DN_BETA),
        "ln_ffn_g": 1.0 + nrm(ks[21], (L, D), 0.02),
        "ln_ffn_b": nrm(ks[22], (L, D), 0.02),
        "w_ple": nrm(ks[23], (L, D_PLE, D), D_PLE ** -0.5 * DN_BETA),
        "w_ple_gate": nrm(ks[24], (L, D, D), D ** -0.5),
        "b_ple_gate": nrm(ks[25], (L, D), 0.02),
        "ln_ple_g": 1.0 + nrm(ks[26], (L, D), 0.02),
        "ln_ple_b": nrm(ks[27], (L, D), 0.02),
    }


def reference(x, p, ln_in_g, ln_in_b, w_in, b_forget, conv_w, conv_b, rg_w_a, rg_b_a,
              rg_w_x, rg_b_x, rg_lambda, w_branch_att, w_branch_rnn, b_merge, w_out,
              ln_mix_g, ln_mix_b, w_ffn_in, w_ffn_out, ln_ffn_g, ln_ffn_b,
              w_ple, w_ple_gate, b_ple_gate, ln_ple_g, ln_ple_b):
    h = layer_norm(x, ln_in_g, ln_in_b)
    for l in range(DEPTH):
        m = hybrid_mixer(h, w_in[l], b_forget[l], conv_w[l], conv_b[l],
                         rg_w_a[l], rg_b_a[l], rg_w_x[l], rg_b_x[l], rg_lambda[l],
                         w_branch_att[l], w_branch_rnn[l], b_merge[l], w_out[l])
        h = layer_norm(DN_ALPHA * h + m, ln_mix_g[l], ln_mix_b[l])
        f = swiglu(h, w_ffn_in[l], w_ffn_out[l])
        h = layer_norm(DN_ALPHA * h + f, ln_ffn_g[l], ln_ffn_b[l])
        e = jax.nn.sigmoid(h @ w_ple_gate[l] + b_ple_gate[l]) * (p[l] @ w_ple[l])
        h = layer_norm(DN_ALPHA * h + e, ln_ple_g[l], ln_ple_b[l])
    return h
```

```python
import functools
import math

import jax
import jax.numpy as jnp
from jax import lax
from jax.experimental import pallas as pl
from jax.experimental.pallas import tpu as pltpu

LN_EPS = 1e-5
RG_C = 8.0
CONV_WIDTH = 4
LOG2E = 1.4426950408889634
LANES = 128
HEAD_ROWS = 16
VMEM_LIMIT_BYTES = 56 * 1024 * 1024

F32 = jnp.float32
BF16 = jnp.bfloat16


def _params(*semantics):
    return pltpu.CompilerParams(dimension_semantics=semantics,
                                vmem_limit_bytes=VMEM_LIMIT_BYTES)


def _layer_spec(tail, l):
    zeros = (0,) * len(tail)
    return pl.BlockSpec((None,) + tuple(tail), lambda *_: (l,) + zeros,
                        pipeline_mode=pl.Buffered(1))


def _const_spec(shape):
    zeros = (0,) * len(shape)
    return pl.BlockSpec(tuple(shape), lambda *_: zeros, pipeline_mode=pl.Buffered(1))


def _sigmoid(x):
    return 1.0 / (1.0 + jnp.exp(-x))


def _layer_norm(x, g, b):
    mu = jnp.mean(x, axis=-1, keepdims=True)
    xc = x - mu
    var = jnp.mean(xc * xc, axis=-1, keepdims=True)
    return xc * lax.rsqrt(var + LN_EPS) * g + b


def _split3(x):
    hi = x.astype(BF16)
    r1 = x - hi.astype(F32)
    mid = r1.astype(BF16)
    lo = (r1 - mid.astype(F32)).astype(BF16)
    return hi, mid, lo


def _ln_in_kernel(x_ref, g_ref, b_ref, o_ref):
    o_ref[...] = _layer_norm(x_ref[...], g_ref[...], b_ref[...])


def _ln_in(x, g, b, ts):
    B, S, D = x.shape
    return pl.pallas_call(
        _ln_in_kernel,
        grid=(B, S // ts),
        in_specs=[pl.BlockSpec((None, ts, D), lambda b_, i: (b_, i, 0)),
                  _const_spec((1, D)), _const_spec((1, D))],
        out_specs=pl.BlockSpec((ts, D), lambda b_, i: (i, b_)),
        out_shape=jax.ShapeDtypeStruct((S, B * D), F32),
        compiler_params=_params("parallel", "parallel"),
        name="ln_in",
    )(x, g.reshape(1, D), b.reshape(1, D))


def _qkv_kernel(h_ref, w_ref, wfc_ref, wfr_ref, bfc_ref, bfr_ref,
                qkv_ref, ccol_ref, crow_ref, carry_c, carry_r, *, n_heads, head_dim, q_scale):
    i = pl.program_id(1)

    @pl.when(i == 0)
    def _():
        carry_c[...] = jnp.zeros_like(carry_c)
        carry_r[...] = jnp.zeros_like(carry_r)

    hb = h_ref[...].astype(BF16)
    tm = hb.shape[0]
    z = jnp.dot(hb, w_ref[...], preferred_element_type=F32)
    for j in range(3 * n_heads):
        blk = z[:, j * head_dim:(j + 1) * head_dim]
        if j < n_heads:
            blk = blk * q_scale
        qkv_ref[j] = blk.astype(BF16)

    def log2_forget(f):
        return (jnp.minimum(f, 0.0) - jnp.log1p(jnp.exp(-jnp.abs(f)))) * LOG2E

    row = lax.broadcasted_iota(jnp.int32, (tm, tm), 0)
    col = lax.broadcasted_iota(jnp.int32, (tm, tm), 1)

    fc = jnp.dot(hb, wfc_ref[...], preferred_element_type=F32) + bfc_ref[...]
    lower = jnp.where(col <= row, 1.0, 0.0).astype(BF16)
    cc = jnp.dot(lower, jnp.concatenate(_split3(log2_forget(fc)), axis=1),
                 preferred_element_type=F32)
    cc = cc[:, :LANES] + cc[:, LANES:2 * LANES] + cc[:, 2 * LANES:] + carry_c[...]
    ccol_ref[...] = cc
    carry_c[...] = cc[tm - 1:tm, :]

    fr = lax.dot_general(wfr_ref[...], hb, (((1,), (1,)), ((), ())),
                         preferred_element_type=F32) + bfr_ref[...][:, :1]
    upper = jnp.where(row <= col, 1.0, 0.0).astype(BF16)
    cr = jnp.dot(jnp.concatenate(_split3(log2_forget(fr)), axis=0), upper,
                 preferred_element_type=F32)
    cr = (cr[:HEAD_ROWS] + cr[HEAD_ROWS:2 * HEAD_ROWS] + cr[2 * HEAD_ROWS:]
          + carry_r[...][:, :1])
    crow_ref[...] = cr
    carry_r[...] = jnp.broadcast_to(cr[:, tm - 1:tm], carry_r.shape)


def _qkv_proj(h2, w_qkv, w_fc, w_fr, b_fc, b_fr, l, *, B, S, D, n_heads, head_dim, tm):
    d_att = n_heads * head_dim
    kern = functools.partial(_qkv_kernel, n_heads=n_heads, head_dim=head_dim,
                             q_scale=LOG2E / math.sqrt(head_dim))
    return pl.pallas_call(
        kern,
        grid=(B, S // tm),
        in_specs=[pl.BlockSpec((tm, D), lambda b_, i: (i, b_)),
                  _layer_spec((D, 3 * d_att), l),
                  _layer_spec((D, LANES), l),
                  _layer_spec((HEAD_ROWS, D), l),
                  _layer_spec((1, LANES), l),
                  _layer_spec((HEAD_ROWS, LANES), l)],
        out_specs=[pl.BlockSpec((3 * n_heads, None, tm, head_dim), lambda b_, i: (0, b_, i, 0)),
                   pl.BlockSpec((None, tm, LANES), lambda b_, i: (b_, i, 0)),
                   pl.BlockSpec((None, HEAD_ROWS, tm), lambda b_, i: (b_, 0, i))],
        out_shape=[jax.ShapeDtypeStruct((3 * n_heads, B, S, head_dim), BF16),
                   jax.ShapeDtypeStruct((B, S, LANES), F32),
                   jax.ShapeDtypeStruct((B, HEAD_ROWS, S), F32)],
        scratch_shapes=[pltpu.VMEM((1, LANES), F32), pltpu.VMEM((HEAD_ROWS, LANES), F32)],
        compiler_params=_params("parallel", "arbitrary"),
        name="qkv_proj",
    )(h2, w_qkv, w_fc, w_fr, b_fc, b_fr)


def _attn_kernel(q_ref, k_ref, v_ref, ccol_ref, crow_ref, o_ref, *, n_heads, tq, tk):
    i = pl.program_id(1)
    band = tq // tk
    n_full = i * band
    lane = lax.broadcasted_iota(jnp.int32, (tq, LANES), 1)
    r_idx = lax.broadcasted_iota(jnp.int32, (tq, tk), 0)
    c_idx = lax.broadcasted_iota(jnp.int32, (tq, tk), 1)
    ccol = ccol_ref[...]

    def head_body(h, _):
        q = q_ref[h]
        c_q = jnp.sum(jnp.where(lane == h, ccol, 0.0), axis=1, keepdims=True)

        def logits(j):
            start = pl.multiple_of(j * tk, tk)
            kj = k_ref[h, pl.ds(start, tk), :]
            vj = v_ref[h, pl.ds(start, tk), :]
            s = lax.dot_general(q, kj, (((1,), (1,)), ((), ())), preferred_element_type=F32)
            return s + c_q - crow_ref[h, j], vj

        s, vj = logits(n_full)
        s = jnp.where(c_idx <= r_idx, s, -jnp.inf)
        m = jnp.max(s, axis=1, keepdims=True)
        p = jnp.exp2(s - m)
        l = jnp.sum(p, axis=1, keepdims=True)
        acc = jnp.dot(p.astype(BF16), vj, preferred_element_type=F32)

        def update(s, vj, m, l, acc):
            m_new = jnp.maximum(m, jnp.max(s, axis=1, keepdims=True))
            alpha = jnp.exp2(m - m_new)
            p = jnp.exp2(s - m_new)
            l = alpha * l + jnp.sum(p, axis=1, keepdims=True)
            acc = alpha * acc + jnp.dot(p.astype(BF16), vj, preferred_element_type=F32)
            return m_new, l, acc

        for d in range(1, band):
            s, vj = logits(n_full + d)
            s = jnp.where(c_idx + d * tk <= r_idx, s, -jnp.inf)
            m, l, acc = update(s, vj, m, l, acc)

        def full_body(j, carry):
            s, vj = logits(j)
            return update(s, vj, *carry)

        m, l, acc = lax.fori_loop(0, n_full, full_body, (m, l, acc))
        o_ref[h] = (acc / l).astype(o_ref.dtype)
        return 0

    lax.fori_loop(0, n_heads, head_body, 0)


def _attention(qkv, ccol, crow5, *, B, S, n_heads, head_dim, tq, tk):
    H = n_heads
    kern = functools.partial(_attn_kernel, n_heads=H, tq=tq, tk=tk)
    return pl.pallas_call(
        kern,
        grid=(B, S // tq),
        in_specs=[pl.BlockSpec((H, None, tq, head_dim), lambda b_, i: (0, b_, i, 0)),
                  pl.BlockSpec((H, None, S, head_dim), lambda b_, i: (1, b_, 0, 0)),
                  pl.BlockSpec((H, None, S, head_dim), lambda b_, i: (2, b_, 0, 0)),
                  pl.BlockSpec((None, tq, LANES), lambda b_, i: (b_, i, 0)),
                  pl.BlockSpec((None, H, S // tk, 1, tk), lambda b_, i: (b_, 0, 0, 0, 0))],
        out_specs=pl.BlockSpec((H, None, tq, head_dim), lambda b_, i: (0, b_, i, 0)),
        out_shape=jax.ShapeDtypeStruct((H, B, S, head_dim), BF16),
        compiler_params=_params("parallel", "arbitrary"),
        name="fox_attention",
    )(qkv, qkv, qkv, ccol, crow5)


def _rnn_kernel(h_ref, w_ref, cw_ref, cb_ref, wax_ref, ba_ref, bx_ref, lam_ref,
                o_ref, rx_ext, a_s, u_s, h_carry, *, n_batch, n_blocks, blk):
    i = pl.program_id(0)
    rows, d_rnn = a_s.shape
    halo = (CONV_WIDTH - 1) * n_batch

    @pl.when(i == 0)
    def _():
        rx_ext[0:halo, :] = jnp.zeros((halo, d_rnn), F32)
        h_carry[...] = jnp.zeros_like(h_carry)

    hb = h_ref[...].astype(BF16)
    z = jnp.dot(hb, w_ref[...], preferred_element_type=F32)
    rx_ext[halo:, :] = z[:, :d_rnn]
    ry = z[:, d_rnn:]

    xc = cb_ref[...] + cw_ref[0:1, :] * rx_ext[0:rows, :]
    for k in range(1, CONV_WIDTH):
        xc = xc + cw_ref[k:k + 1, :] * rx_ext[k * n_batch:k * n_batch + rows, :]
    rx_ext[0:halo, :] = rx_ext[rows:rows + halo, :]
    xcb = xc.astype(BF16)

    log_a_max = -RG_C * (jnp.maximum(-lam_ref[...], 0.0)
                         + jnp.log1p(jnp.exp(-jnp.abs(lam_ref[...]))))
    for n in range(n_blocks):
        sl = slice(n * blk, (n + 1) * blk)
        g = jnp.dot(xcb[:, sl], wax_ref[n], preferred_element_type=F32)
        r = _sigmoid(g[:, :blk] + ba_ref[:, sl])
        gi = _sigmoid(g[:, blk:] + bx_ref[:, sl])
        log_a = log_a_max[:, sl] * r
        a = jnp.exp(log_a)
        a_s[:, sl] = a
        u_s[:, sl] = jnp.sqrt(1.0 - a * a) * (gi * xc[:, sl])

    def step(t, h):
        r0 = pl.multiple_of(t * n_batch, n_batch)
        h = a_s[pl.ds(r0, n_batch), :] * h + u_s[pl.ds(r0, n_batch), :]
        u_s[pl.ds(r0, n_batch), :] = h
        return h

    h_carry[...] = lax.fori_loop(0, rows // n_batch, step, h_carry[...], unroll=4)
    gelu = 0.5 * ry * (1.0 + jnp.tanh(math.sqrt(2.0 / math.pi) * (ry + 0.044715 * (ry * ry * ry))))
    o_ref[...] = (u_s[...] * gelu).astype(o_ref.dtype)


def _rnn_branch(h_rows, w_r, conv_w, conv_b, w_ax, b_a, b_x, lam, l, *, B, S, D, n_blocks, tt):
    rows = tt * B
    blk = D // n_blocks
    kern = functools.partial(_rnn_kernel, n_batch=B, n_blocks=n_blocks, blk=blk)
    return pl.pallas_call(
        kern,
        grid=(S // tt,),
        in_specs=[pl.BlockSpec((rows, D), lambda i: (i, 0)),
                  _layer_spec((D, 2 * D), l),
                  _layer_spec((CONV_WIDTH, D), l),
                  _layer_spec((1, D), l),
                  _layer_spec((n_blocks, blk, 2 * blk), l),
                  _layer_spec((1, D), l), _layer_spec((1, D), l), _layer_spec((1, D), l)],
        out_specs=pl.BlockSpec((rows, D), lambda i: (i, 0)),
        out_shape=jax.ShapeDtypeStruct((S * B, D), BF16),
        scratch_shapes=[pltpu.VMEM((rows + (CONV_WIDTH - 1) * B, D), F32),
                        pltpu.VMEM((rows, D), F32), pltpu.VMEM((rows, D), F32),
                        pltpu.VMEM((B, D), F32)],
        compiler_params=_params("arbitrary"),
        name="rglru_branch",
    )(h_rows, w_r, conv_w, conv_b, w_ax, b_a, b_x, lam)


def _merge_kernel(h_ref, att_ref, rnn_ref, wg_ref, wa_ref, wr_ref, wo_ref, bm_ref, g_ref, b_ref,
                  o_ref, *, n_heads, alpha):
    h = h_ref[...]
    d = h.shape[1]
    gates = jnp.dot(h.astype(BF16), wg_ref[...], preferred_element_type=F32)
    att = jnp.concatenate([att_ref[j] for j in range(n_heads)], axis=1)
    ya = jnp.dot(att, wa_ref[...], preferred_element_type=F32)
    yb = jnp.dot(rnn_ref[...], wr_ref[...], preferred_element_type=F32)
    merged = (_sigmoid(gates[:, :d] + bm_ref[0:1, :]) * ya
              + _sigmoid(gates[:, d:] + bm_ref[1:2, :]) * yb)
    m = jnp.dot(merged.astype(BF16), wo_ref[...], preferred_element_type=F32)
    o_ref[...] = _layer_norm(alpha * h + m, g_ref[...], b_ref[...])


def _merge(h2, att, rnn2, w_g, w_a, w_r, w_o, b_m, ln_g, ln_b, l, *, B, S, D, n_heads, head_dim,
           ts, alpha):
    kern = functools.partial(_merge_kernel, n_heads=n_heads, alpha=alpha)
    d_att = n_heads * head_dim
    tok = pl.BlockSpec((ts, D), lambda b_, i: (i, b_))
    return pl.pallas_call(
        kern,
        grid=(B, S // ts),
        in_specs=[tok,
                  pl.BlockSpec((n_heads, None, ts, head_dim), lambda b_, i: (0, b_, i, 0)),
                  tok,
                  _layer_spec((D, 2 * D), l), _layer_spec((d_att, D), l),
                  _layer_spec((D, D), l), _layer_spec((D, D), l),
                  _layer_spec((2, D), l), _layer_spec((1, D), l), _layer_spec((1, D), l)],
        out_specs=tok,
        out_shape=jax.ShapeDtypeStruct((S, B * D), F32),
        compiler_params=_params("parallel", "parallel"),
        name="merge_out",
    )(h2, att, rnn2, w_g, w_a, w_r, w_o, b_m, ln_g, ln_b)


def _ffn_kernel(h_ref, p_ref, wi_ref, wo_ref, g1_ref, b1_ref, wpg_ref, bpg_ref, wp_ref,
                g2_ref, b2_ref, o_ref, *, d_ff, chunks, alpha):
    h = h_ref[...]
    hb = h.astype(BF16)
    f = None
    for lo, hi in chunks:
        hg = jnp.dot(hb, wi_ref[:, lo:hi], preferred_element_type=F32)
        hu = jnp.dot(hb, wi_ref[:, d_ff + lo:d_ff + hi], preferred_element_type=F32)
        act = (hg * _sigmoid(hg) * hu).astype(BF16)
        part = jnp.dot(act, wo_ref[lo:hi, :], preferred_element_type=F32)
        f = part if f is None else f + part
    h = _layer_norm(alpha * h + f, g1_ref[...], b1_ref[...])
    gate = _sigmoid(jnp.dot(h.astype(BF16), wpg_ref[...], preferred_element_type=F32) + bpg_ref[...])
    e = gate * jnp.dot(p_ref[...].astype(BF16), wp_ref[...], preferred_element_type=F32)
    o_ref[...] = _layer_norm(alpha * h + e, g2_ref[...], b2_ref[...])


def _ffn_chunks(d_ff, width):
    edges = list(range(0, d_ff, width)) + [d_ff]
    return tuple(zip(edges[:-1], edges[1:]))


def _ffn_ple(h2, p, w_i, w_o, g1, b1, w_pg, b_pg, w_p, g2, b2, l, *, B, S, D, d_ff, d_ple, ts,
             alpha, batch_major_out):
    kern = functools.partial(_ffn_kernel, d_ff=d_ff, chunks=_ffn_chunks(d_ff, 1024), alpha=alpha)
    tok = pl.BlockSpec((ts, D), lambda b_, i: (i, b_))
    if batch_major_out:
        out_spec = pl.BlockSpec((None, ts, D), lambda b_, i: (b_, i, 0))
        out_shape = jax.ShapeDtypeStruct((B, S, D), F32)
    else:
        out_spec, out_shape = tok, jax.ShapeDtypeStruct((S, B * D), F32)
    vec = _layer_spec((1, D), l)
    return pl.pallas_call(
        kern,
        grid=(B, S // ts),
        in_specs=[tok,
                  pl.BlockSpec((None, None, ts, d_ple), lambda b_, i: (l, b_, i, 0)),
                  _layer_spec((D, 2 * d_ff), l), _layer_spec((d_ff, D), l), vec, vec,
                  _layer_spec((D, D), l), vec, _layer_spec((d_ple, D), l), vec, vec],
        out_specs=out_spec,
        out_shape=out_shape,
        compiler_params=_params("parallel", "parallel"),
        name="ffn_ple",
    )(h2, p, w_i, w_o, g1, b1, w_pg, b_pg, w_p, g2, b2)


def kernel(x, p, ln_in_g, ln_in_b, w_in, b_forget, conv_w, conv_b, rg_w_a, rg_b_a, rg_w_x, rg_b_x,
           rg_lambda, w_branch_att, w_branch_rnn, b_merge, w_out, ln_mix_g, ln_mix_b, w_ffn_in,
           w_ffn_out, ln_ffn_g, ln_ffn_b, w_ple, w_ple_gate, b_ple_gate, ln_ple_g, ln_ple_b):
    B, S, D = x.shape
    L = w_in.shape[0]
    H = b_forget.shape[1]
    d_att = w_branch_att.shape[1]
    head_dim = d_att // H
    n_blocks, blk = rg_w_a.shape[1], rg_w_a.shape[2]
    d_ff = w_ffn_out.shape[1]
    d_ple = w_ple.shape[1]
    assert head_dim == LANES and blk == LANES and D == n_blocks * blk and d_att == D
    assert H <= 8 and B % 8 == 0
    alpha = float((2 * L) ** 0.25)

    ts = min(512, S)
    tq = tk = min(256, S)
    tt = max(1, min(S, 512 // B))

    o_f, o_rx = 3 * d_att, 3 * d_att + H
    o_g = o_rx + 2 * D
    w_qkv = w_in[:, :, :o_f].astype(BF16)
    w_f = w_in[:, :, o_f:o_rx]
    w_fc = jnp.pad(w_f, ((0, 0), (0, 0), (0, LANES - H))).astype(BF16)
    w_fr = jnp.pad(jnp.swapaxes(w_f, 1, 2), ((0, 0), (0, HEAD_ROWS - H), (0, 0))).astype(BF16)
    b_fc = jnp.pad(b_forget, ((0, 0), (0, LANES - H))).reshape(L, 1, LANES)
    b_fr = jnp.broadcast_to(jnp.pad(b_forget, ((0, 0), (0, HEAD_ROWS - H)))[:, :, None],
                            (L, HEAD_ROWS, LANES))
    w_r = w_in[:, :, o_rx:o_g].astype(BF16)
    w_g = w_in[:, :, o_g:].astype(BF16)
    w_ax = jnp.concatenate([rg_w_a, rg_w_x], axis=-1).astype(BF16)
    w_ba, w_br, w_o = (w.astype(BF16) for w in (w_branch_att, w_branch_rnn, w_out))
    w_fi, w_fo = w_ffn_in.astype(BF16), w_ffn_out.astype(BF16)
    w_pg, w_p = w_ple_gate.astype(BF16), w_ple.astype(BF16)
    vec = lambda a: a.reshape(L, 1, D)

    h2 = _ln_in(x, ln_in_g, ln_in_b, ts)
    for l in range(L):
        qkv, ccol, crow = _qkv_proj(h2, w_qkv, w_fc, w_fr, b_fc, b_fr, l, B=B, S=S, D=D,
                                    n_heads=H, head_dim=head_dim, tm=ts)
        crow5 = crow[:, :H, :].reshape(B, H, S // tk, 1, tk)
        att = _attention(qkv, ccol, crow5, B=B, S=S, n_heads=H, head_dim=head_dim, tq=tq, tk=tk)
        rnn = _rnn_branch(h2.reshape(S * B, D), w_r, conv_w, vec(conv_b), w_ax, vec(rg_b_a),
                          vec(rg_b_x), vec(rg_lambda), l, B=B, S=S, D=D, n_blocks=n_blocks, tt=tt)
        h2 = _merge(h2, att, rnn.reshape(S, B * D), w_g, w_ba, w_br, w_o, b_merge,
                    vec(ln_mix_g), vec(ln_mix_b), l, B=B, S=S, D=D, n_heads=H, head_dim=head_dim,
                    ts=ts, alpha=alpha)
        h2 = _ffn_ple(h2, p, w_fi, w_fo, vec(ln_ffn_g), vec(ln_ffn_b), w_pg, vec(b_ple_gate), w_p,
                      vec(ln_ple_g), vec(ln_ple_b), l, B=B, S=S, D=D, d_ff=d_ff, d_ple=d_ple,
                      ts=ts, alpha=alpha, batch_major_out=(l == L - 1))
    return h2
```

```python
import functools
import math

import jax
import jax.numpy as jnp
from jax import lax
from jax.experimental import pallas as pl
from jax.experimental.pallas import tpu as pltpu

LN_EPS = 1e-5
RG_C = 8.0
CONV_WIDTH = 4
LOG2E = 1.4426950408889634
LANES = 128
KEY_TERM_SHIFT = 3
KEY_TERM_STRIDE = 1 << KEY_TERM_SHIFT
VMEM_LIMIT_BYTES = 56 * 1024 * 1024

F32 = jnp.float32
BF16 = jnp.bfloat16


def _params(*semantics):
    return pltpu.CompilerParams(dimension_semantics=semantics,
                                vmem_limit_bytes=VMEM_LIMIT_BYTES)


def _layer_spec(tail, l):
    zeros = (0,) * len(tail)
    return pl.BlockSpec((None,) + tuple(tail), lambda *_: (l,) + zeros,
                        pipeline_mode=pl.Buffered(1))


def _const_spec(shape):
    zeros = (0,) * len(shape)
    return pl.BlockSpec(tuple(shape), lambda *_: zeros, pipeline_mode=pl.Buffered(1))


def _sigmoid(x):
    return 1.0 / (1.0 + jnp.exp(-x))


def _layer_norm(x, g, b):
    mu = jnp.mean(x, axis=-1, keepdims=True)
    xc = x - mu
    var = jnp.mean(xc * xc, axis=-1, keepdims=True)
    return xc * lax.rsqrt(var + LN_EPS) * g + b


def _split3(x):
    hi = x.astype(BF16)
    r1 = x - hi.astype(F32)
    mid = r1.astype(BF16)
    lo = (r1 - mid.astype(F32)).astype(BF16)
    return hi, mid, lo


def _ln_in_kernel(x_ref, g_ref, b_ref, o_ref):
    o_ref[...] = _layer_norm(x_ref[...], g_ref[...], b_ref[...])


def _ln_in(x, g, b, ts):
    B, S, D = x.shape
    return pl.pallas_call(
        _ln_in_kernel,
        grid=(B, S // ts),
        in_specs=[pl.BlockSpec((None, ts, D), lambda b_, i: (b_, i, 0)),
                  _const_spec((1, D)), _const_spec((1, D))],
        out_specs=pl.BlockSpec((ts, D), lambda b_, i: (i, b_)),
        out_shape=jax.ShapeDtypeStruct((S, B * D), F32),
        compiler_params=_params("parallel", "parallel"),
        name="ln_in",
    )(x, g.reshape(1, D), b.reshape(1, D))


def _qkv_kernel(h_ref, w_ref, wfc_ref, bfc_ref, qkv_ref, ccol_ref, kf_ref, carry_c,
                *, n_heads, head_dim, q_scale):
    i = pl.program_id(1)

    @pl.when(i == 0)
    def _():
        carry_c[...] = jnp.zeros_like(carry_c)

    hb = h_ref[...].astype(BF16)
    tm = hb.shape[0]
    z = jnp.dot(hb, w_ref[...], preferred_element_type=F32)
    for j in range(3 * n_heads):
        blk = z[:, j * head_dim:(j + 1) * head_dim]
        if j < n_heads:
            blk = blk * q_scale
        qkv_ref[j] = blk.astype(BF16)

    def log2_forget(f):
        return (jnp.minimum(f, 0.0) - jnp.log1p(jnp.exp(-jnp.abs(f)))) * LOG2E

    row = lax.broadcasted_iota(jnp.int32, (tm, tm), 0)
    col = lax.broadcasted_iota(jnp.int32, (tm, tm), 1)

    fc = jnp.dot(hb, wfc_ref[...], preferred_element_type=F32) + bfc_ref[...]
    lower = jnp.where(col <= row, 1.0, 0.0).astype(BF16)
    cc = jnp.dot(lower, jnp.concatenate(_split3(log2_forget(fc)), axis=1),
                 preferred_element_type=F32)
    cc = cc[:, :LANES] + cc[:, LANES:2 * LANES] + cc[:, 2 * LANES:] + carry_c[...]
    ccol_ref[...] = cc
    carry_c[...] = cc[tm - 1:tm, :]

    r = lax.broadcasted_iota(jnp.int32, (3 * LANES, LANES), 0)
    c = lax.broadcasted_iota(jnp.int32, (3 * LANES, LANES), 1)
    src = lax.shift_right_logical(c, KEY_TERM_SHIFT) * LANES + (c & (KEY_TERM_STRIDE - 1))
    place = jnp.where(c < 3 * KEY_TERM_STRIDE, jnp.where(r == src, 1.0, 0.0), 0.0).astype(BF16)
    kf_ref[...] = jnp.dot(jnp.concatenate(_split3(-cc), axis=1), place,
                          preferred_element_type=F32).astype(BF16)


def _qkv_proj(h2, w_qkv, w_fc, b_fc, l, *, B, S, D, n_heads, head_dim, tm):
    d_att = n_heads * head_dim
    kern = functools.partial(_qkv_kernel, n_heads=n_heads, head_dim=head_dim,
                             q_scale=LOG2E / math.sqrt(head_dim))
    return pl.pallas_call(
        kern,
        grid=(B, S // tm),
        in_specs=[pl.BlockSpec((tm, D), lambda b_, i: (i, b_)),
                  _layer_spec((D, 3 * d_att), l),
                  _layer_spec((D, LANES), l),
                  _layer_spec((1, LANES), l)],
        out_specs=[pl.BlockSpec((3 * n_heads, None, tm, head_dim), lambda b_, i: (0, b_, i, 0)),
                   pl.BlockSpec((None, tm, LANES), lambda b_, i: (b_, i, 0)),
                   pl.BlockSpec((None, tm, LANES), lambda b_, i: (b_, i, 0))],
        out_shape=[jax.ShapeDtypeStruct((3 * n_heads, B, S, head_dim), BF16),
                   jax.ShapeDtypeStruct((B, S, LANES), F32),
                   jax.ShapeDtypeStruct((B, S, LANES), BF16)],
        scratch_shapes=[pltpu.VMEM((1, LANES), F32)],
        compiler_params=_params("parallel", "arbitrary"),
        name="qkv_proj",
    )(h2, w_qkv, w_fc, b_fc)


def _attn_kernel(q_ref, k_ref, v_ref, cq_ref, kf_ref, o_ref, *, n_heads, tq, group):
    i = pl.program_id(1)
    lane = lax.broadcasted_iota(jnp.int32, (tq, LANES), 1)
    r_idx = lax.broadcasted_iota(jnp.int32, (tq, tq), 0)
    c_idx = lax.broadcasted_iota(jnp.int32, (tq, tq), 1)
    causal_bias = jnp.where(c_idx <= r_idx, 0.0, -jnp.inf)
    cq_all = cq_ref[...]
    nt = (((1,), (1,)), ((), ()))

    def chain_inputs(h):
        pick = jnp.where(lane < 3 * KEY_TERM_STRIDE,
                         jnp.where((lane & (KEY_TERM_STRIDE - 1)) == h, 1.0, 0.0), 0.0)
        q_aug = jnp.concatenate([q_ref[h], pick.astype(BF16)], axis=1)
        c_q = jnp.sum(jnp.where(lane == h, cq_all, 0.0), axis=1, keepdims=True)
        return q_aug, c_q

    def tile(h, q_aug, j):
        start = pl.multiple_of(j * tq, tq)
        k_aug = jnp.concatenate([k_ref[h, pl.ds(start, tq), :], kf_ref[pl.ds(start, tq), :]], axis=1)
        s = lax.dot_general(q_aug, k_aug, nt, preferred_element_type=F32)
        return s, v_ref[h, pl.ds(start, tq), :]

    def head_group(g, _):
        heads = [g * group + u for u in range(group)]
        inputs = [chain_inputs(h) for h in heads]

        state = []
        for h, (q_aug, c_q) in zip(heads, inputs):
            s, vj = tile(h, q_aug, i)
            s = s + causal_bias
            m = c_q + jnp.max(s, axis=1, keepdims=True)
            p = jnp.exp2(s + (c_q - m))
            state += [m, jnp.sum(p, axis=1, keepdims=True),
                      jnp.dot(p.astype(BF16), vj, preferred_element_type=F32)]

        def full_tiles(j, state):
            new = []
            for u, (h, (q_aug, c_q)) in enumerate(zip(heads, inputs)):
                m, l, acc = state[3 * u:3 * u + 3]
                s, vj = tile(h, q_aug, j)
                m_new = jnp.maximum(m, c_q + jnp.max(s, axis=1, keepdims=True))
                alpha = jnp.exp2(m - m_new)
                p = jnp.exp2(s + (c_q - m_new))
                new += [m_new, alpha * l + jnp.sum(p, axis=1, keepdims=True),
                        alpha * acc + jnp.dot(p.astype(BF16), vj, preferred_element_type=F32)]
            return tuple(new)

        state = lax.fori_loop(0, i, full_tiles, tuple(state))
        for u, h in enumerate(heads):
            o_ref[h] = (state[3 * u + 2] / state[3 * u + 1]).astype(o_ref.dtype)
        return 0

    lax.fori_loop(0, n_heads // group, head_group, 0)


def _attention(qkv, ccol, kf, *, B, S, n_heads, head_dim, tq, group):
    H = n_heads
    kern = functools.partial(_attn_kernel, n_heads=H, tq=tq, group=group)
    return pl.pallas_call(
        kern,
        grid=(B, S // tq),
        in_specs=[pl.BlockSpec((H, None, tq, head_dim), lambda b_, i: (0, b_, i, 0)),
                  pl.BlockSpec((H, None, S, head_dim), lambda b_, i: (1, b_, 0, 0)),
                  pl.BlockSpec((H, None, S, head_dim), lambda b_, i: (2, b_, 0, 0)),
                  pl.BlockSpec((None, tq, LANES), lambda b_, i: (b_, i, 0)),
                  pl.BlockSpec((None, S, LANES), lambda b_, i: (b_, 0, 0))],
        out_specs=pl.BlockSpec((H, None, tq, head_dim), lambda b_, i: (0, b_, i, 0)),
        out_shape=jax.ShapeDtypeStruct((H, B, S, head_dim), BF16),
        compiler_params=_params("parallel", "arbitrary"),
        name="fox_attention",
    )(qkv, qkv, qkv, ccol, kf)


def _rnn_kernel(h_ref, w_ref, cw_ref, cb_ref, wax_ref, ba_ref, bx_ref, lam_ref,
                o_ref, rx_ext, a_s, u_s, h_carry, *, n_batch, n_blocks, blk):
    i = pl.program_id(0)
    rows, d_rnn = a_s.shape
    halo = (CONV_WIDTH - 1) * n_batch

    @pl.when(i == 0)
    def _():
        rx_ext[0:halo, :] = jnp.zeros((halo, d_rnn), F32)
        h_carry[...] = jnp.zeros_like(h_carry)

    hb = h_ref[...].astype(BF16)
    z = jnp.dot(hb, w_ref[...], preferred_element_type=F32)
    rx_ext[halo:, :] = z[:, :d_rnn]
    ry = z[:, d_rnn:]

    xc = cb_ref[...] + cw_ref[0:1, :] * rx_ext[0:rows, :]
    for k in range(1, CONV_WIDTH):
        xc = xc + cw_ref[k:k + 1, :] * rx_ext[k * n_batch:k * n_batch + rows, :]
    rx_ext[0:halo, :] = rx_ext[rows:rows + halo, :]
    xcb = xc.astype(BF16)

    log_a_max = -RG_C * (jnp.maximum(-lam_ref[...], 0.0)
                         + jnp.log1p(jnp.exp(-jnp.abs(lam_ref[...]))))
    for n in range(n_blocks):
        sl = slice(n * blk, (n + 1) * blk)
        g = jnp.dot(xcb[:, sl], wax_ref[n], preferred_element_type=F32)
        r = _sigmoid(g[:, :blk] + ba_ref[:, sl])
        gi = _sigmoid(g[:, blk:] + bx_ref[:, sl])
        log_a = log_a_max[:, sl] * r
        a = jnp.exp(log_a)
        a_s[:, sl] = a
        u_s[:, sl] = jnp.sqrt(1.0 - a * a) * (gi * xc[:, sl])

    def step(t, h):
        r0 = pl.multiple_of(t * n_batch, n_batch)
        h = a_s[pl.ds(r0, n_batch), :] * h + u_s[pl.ds(r0, n_batch), :]
        u_s[pl.ds(r0, n_batch), :] = h
        return h

    h_carry[...] = lax.fori_loop(0, rows // n_batch, step, h_carry[...], unroll=4)
    gelu = 0.5 * ry * (1.0 + jnp.tanh(math.sqrt(2.0 / math.pi) * (ry + 0.044715 * (ry * ry * ry))))
    o_ref[...] = (u_s[...] * gelu).astype(o_ref.dtype)


def _rnn_branch(h_rows, w_r, conv_w, conv_b, w_ax, b_a, b_x, lam, l, *, B, S, D, n_blocks, tt):
    rows = tt * B
    blk = D // n_blocks
    kern = functools.partial(_rnn_kernel, n_batch=B, n_blocks=n_blocks, blk=blk)
    return pl.pallas_call(
        kern,
        grid=(S // tt,),
        in_specs=[pl.BlockSpec((rows, D), lambda i: (i, 0)),
                  _layer_spec((D, 2 * D), l),
                  _layer_spec((CONV_WIDTH, D), l),
                  _layer_spec((1, D), l),
                  _layer_spec((n_blocks, blk, 2 * blk), l),
                  _layer_spec((1, D), l), _layer_spec((1, D), l), _layer_spec((1, D), l)],
        out_specs=pl.BlockSpec((rows, D), lambda i: (i, 0)),
        out_shape=jax.ShapeDtypeStruct((S * B, D), BF16),
        scratch_shapes=[pltpu.VMEM((rows + (CONV_WIDTH - 1) * B, D), F32),
                        pltpu.VMEM((rows, D), F32), pltpu.VMEM((rows, D), F32),
                        pltpu.VMEM((B, D), F32)],
        compiler_params=_params("arbitrary"),
        name="rglru_branch",
    )(h_rows, w_r, conv_w, conv_b, w_ax, b_a, b_x, lam)


def _merge_kernel(h_ref, att_ref, rnn_ref, wg_ref, wa_ref, wr_ref, wo_ref, bm_ref, g_ref, b_ref,
                  o_ref, *, n_heads, alpha):
    h = h_ref[...]
    d = h.shape[1]
    gates = jnp.dot(h.astype(BF16), wg_ref[...], preferred_element_type=F32)
    att = jnp.concatenate([att_ref[j] for j in range(n_heads)], axis=1)
    ya = jnp.dot(att, wa_ref[...], preferred_element_type=F32)
    yb = jnp.dot(rnn_ref[...], wr_ref[...], preferred_element_type=F32)
    merged = (_sigmoid(gates[:, :d] + bm_ref[0:1, :]) * ya
              + _sigmoid(gates[:, d:] + bm_ref[1:2, :]) * yb)
    m = jnp.dot(merged.astype(BF16), wo_ref[...], preferred_element_type=F32)
    o_ref[...] = _layer_norm(alpha * h + m, g_ref[...], b_ref[...])


def _merge(h2, att, rnn2, w_g, w_a, w_r, w_o, b_m, ln_g, ln_b, l, *, B, S, D, n_heads, head_dim,
           ts, alpha):
    kern = functools.partial(_merge_kernel, n_heads=n_heads, alpha=alpha)
    d_att = n_heads * head_dim
    tok = pl.BlockSpec((ts, D), lambda b_, i: (i, b_))
    return pl.pallas_call(
        kern,
        grid=(B, S // ts),
        in_specs=[tok,
                  pl.BlockSpec((n_heads, None, ts, head_dim), lambda b_, i: (0, b_, i, 0)),
                  tok,
                  _layer_spec((D, 2 * D), l), _layer_spec((d_att, D), l),
                  _layer_spec((D, D), l), _layer_spec((D, D), l),
                  _layer_spec((2, D), l), _layer_spec((1, D), l), _layer_spec((1, D), l)],
        out_specs=tok,
        out_shape=jax.ShapeDtypeStruct((S, B * D), F32),
        compiler_params=_params("parallel", "parallel"),
        name="merge_out",
    )(h2, att, rnn2, w_g, w_a, w_r, w_o, b_m, ln_g, ln_b)


def _ffn_kernel(h_ref, p_ref, wi_ref, wo_ref, g1_ref, b1_ref, wpg_ref, bpg_ref, wp_ref,
                g2_ref, b2_ref, o_ref, *, d_ff, chunks, alpha):
    h = h_ref[...]
    hb = h.astype(BF16)
    f = None
    for lo, hi in chunks:
        hg = jnp.dot(hb, wi_ref[:, lo:hi], preferred_element_type=F32)
        hu = jnp.dot(hb, wi_ref[:, d_ff + lo:d_ff + hi], preferred_element_type=F32)
        act = (hg * _sigmoid(hg) * hu).astype(BF16)
        part = jnp.dot(act, wo_ref[lo:hi, :], preferred_element_type=F32)
        f = part if f is None else f + part
    h = _layer_norm(alpha * h + f, g1_ref[...], b1_ref[...])
    gate = _sigmoid(jnp.dot(h.astype(BF16), wpg_ref[...], preferred_element_type=F32) + bpg_ref[...])
    e = gate * jnp.dot(p_ref[...].astype(BF16), wp_ref[...], preferred_element_type=F32)
    o_ref[...] = _layer_norm(alpha * h + e, g2_ref[...], b2_ref[...])


def _ffn_chunks(d_ff, width):
    edges = list(range(0, d_ff, width)) + [d_ff]
    return tuple(zip(edges[:-1], edges[1:]))


def _ffn_ple(h2, p, w_i, w_o, g1, b1, w_pg, b_pg, w_p, g2, b2, l, *, B, S, D, d_ff, d_ple, ts,
             alpha, batch_major_out):
    kern = functools.partial(_ffn_kernel, d_ff=d_ff, chunks=_ffn_chunks(d_ff, 1024), alpha=alpha)
    tok = pl.BlockSpec((ts, D), lambda b_, i: (i, b_))
    if batch_major_out:
        out_spec = pl.BlockSpec((None, ts, D), lambda b_, i: (b_, i, 0))
        out_shape = jax.ShapeDtypeStruct((B, S, D), F32)
    else:
        out_spec, out_shape = tok, jax.ShapeDtypeStruct((S, B * D), F32)
    vec = _layer_spec((1, D), l)
    return pl.pallas_call(
        kern,
        grid=(B, S // ts),
        in_specs=[tok,
                  pl.BlockSpec((None, None, ts, d_ple), lambda b_, i: (l, b_, i, 0)),
                  _layer_spec((D, 2 * d_ff), l), _layer_spec((d_ff, D), l), vec, vec,
                  _layer_spec((D, D), l), vec, _layer_spec((d_ple, D), l), vec, vec],
        out_specs=out_spec,
        out_shape=out_shape,
        compiler_params=_params("parallel", "parallel"),
        name="ffn_ple",
    )(h2, p, w_i, w_o, g1, b1, w_pg, b_pg, w_p, g2, b2)


def kernel(x, p, ln_in_g, ln_in_b, w_in, b_forget, conv_w, conv_b, rg_w_a, rg_b_a, rg_w_x, rg_b_x,
           rg_lambda, w_branch_att, w_branch_rnn, b_merge, w_out, ln_mix_g, ln_mix_b, w_ffn_in,
           w_ffn_out, ln_ffn_g, ln_ffn_b, w_ple, w_ple_gate, b_ple_gate, ln_ple_g, ln_ple_b):
    B, S, D = x.shape
    L = w_in.shape[0]
    H = b_forget.shape[1]
    d_att = w_branch_att.shape[1]
    head_dim = d_att // H
    n_blocks, blk = rg_w_a.shape[1], rg_w_a.shape[2]
    d_ff = w_ffn_out.shape[1]
    d_ple = w_ple.shape[1]
    assert head_dim == LANES and blk == LANES and D == n_blocks * blk and d_att == D
    assert H <= KEY_TERM_STRIDE and H % 2 == 0 and B % 8 == 0
    alpha = float((2 * L) ** 0.25)

    ts = min(512, S)
    tq = min(512, S)
    tt = max(1, min(S, 512 // B))

    o_f, o_rx = 3 * d_att, 3 * d_att + H
    o_g = o_rx + 2 * D
    w_qkv = w_in[:, :, :o_f].astype(BF16)
    w_f = w_in[:, :, o_f:o_rx]
    w_fc = jnp.pad(w_f, ((0, 0), (0, 0), (0, LANES - H))).astype(BF16)
    b_fc = jnp.pad(b_forget, ((0, 0), (0, LANES - H))).reshape(L, 1, LANES)
    w_r = w_in[:, :, o_rx:o_g].astype(BF16)
    w_g = w_in[:, :, o_g:].astype(BF16)
    w_ax = jnp.concatenate([rg_w_a, rg_w_x], axis=-1).astype(BF16)
    w_ba, w_br, w_o = (w.astype(BF16) for w in (w_branch_att, w_branch_rnn, w_out))
    w_fi, w_fo = w_ffn_in.astype(BF16), w_ffn_out.astype(BF16)
    w_pg, w_p = w_ple_gate.astype(BF16), w_ple.astype(BF16)
    vec = lambda a: a.reshape(L, 1, D)

    h2 = _ln_in(x, ln_in_g, ln_in_b, ts)
    for l in range(L):
        qkv, ccol, kf = _qkv_proj(h2, w_qkv, w_fc, b_fc, l, B=B, S=S, D=D,
                                  n_heads=H, head_dim=head_dim, tm=ts)
        att = _attention(qkv, ccol, kf, B=B, S=S, n_heads=H, head_dim=head_dim, tq=tq, group=2)
        rnn = _rnn_branch(h2.reshape(S * B, D), w_r, conv_w, vec(conv_b), w_ax, vec(rg_b_a),
                          vec(rg_b_x), vec(rg_lambda), l, B=B, S=S, D=D, n_blocks=n_blocks, tt=tt)
        h2 = _merge(h2, att, rnn.reshape(S, B * D), w_g, w_ba, w_br, w_o, b_merge,
                    vec(ln_mix_g), vec(ln_mix_b), l, B=B, S=S, D=D, n_heads=H, head_dim=head_dim,
                    ts=ts, alpha=alpha)
        h2 = _ffn_ple(h2, p, w_fi, w_fo, vec(ln_ffn_g), vec(ln_ffn_b), w_pg, vec(b_ple_gate), w_p,
                      vec(ln_ple_g), vec(ln_ple_b), l, B=B, S=S, D=D, d_ff=d_ff, d_ple=d_ple,
                      ts=ts, alpha=alpha, batch_major_out=(l == L - 1))
    return h2
```

```python
import functools
import math

import jax
import jax.numpy as jnp
from jax import lax
from jax.experimental import pallas as pl
from jax.experimental.pallas import tpu as pltpu

LN_EPS = 1e-5
RG_C = 8.0
CONV_WIDTH = 4
LOG2E = 1.4426950408889634
LANES = 128
SUBLANES = 8
KEY_TERM_SHIFT = 3
KEY_TERM_STRIDE = 1 << KEY_TERM_SHIFT
VMEM_LIMIT_BYTES = 56 * 1024 * 1024

F32 = jnp.float32
BF16 = jnp.bfloat16


def _params(*semantics):
    return pltpu.CompilerParams(dimension_semantics=semantics,
                                vmem_limit_bytes=VMEM_LIMIT_BYTES)


def _layer_spec(tail, l):
    zeros = (0,) * len(tail)
    return pl.BlockSpec((None,) + tuple(tail), lambda *_: (l,) + zeros,
                        pipeline_mode=pl.Buffered(1))


def _const_spec(shape):
    zeros = (0,) * len(shape)
    return pl.BlockSpec(tuple(shape), lambda *_: zeros, pipeline_mode=pl.Buffered(1))


def _sigmoid(x):
    return 1.0 / (1.0 + jnp.exp(-x))


def _layer_norm(x, g, b):
    mu = jnp.mean(x, axis=-1, keepdims=True)
    xc = x - mu
    var = jnp.mean(xc * xc, axis=-1, keepdims=True)
    return xc * lax.rsqrt(var + LN_EPS) * g + b


def _split3(x):
    hi = x.astype(BF16)
    r1 = x - hi.astype(F32)
    mid = r1.astype(BF16)
    lo = (r1 - mid.astype(F32)).astype(BF16)
    return hi, mid, lo


def _ln_in_kernel(x_ref, g_ref, b_ref, o_ref):
    o_ref[...] = _layer_norm(x_ref[...], g_ref[...], b_ref[...])


def _ln_in(x, g, b, ts):
    B, S, D = x.shape
    return pl.pallas_call(
        _ln_in_kernel,
        grid=(B, S // ts),
        in_specs=[pl.BlockSpec((None, ts, D), lambda b_, i: (b_, i, 0)),
                  _const_spec((1, D)), _const_spec((1, D))],
        out_specs=pl.BlockSpec((ts, D), lambda b_, i: (i, b_)),
        out_shape=jax.ShapeDtypeStruct((S, B * D), F32),
        compiler_params=_params("parallel", "parallel"),
        name="ln_in",
    )(x, g.reshape(1, D), b.reshape(1, D))


def _qkv_kernel(h_ref, w_ref, wfc_ref, bfc_ref, qkv_ref, ccol_ref, kf_ref, carry_c,
                *, n_heads, head_dim, q_scale):
    i = pl.program_id(1)

    @pl.when(i == 0)
    def _():
        carry_c[...] = jnp.zeros_like(carry_c)

    hb = h_ref[...].astype(BF16)
    tm = hb.shape[0]
    z = jnp.dot(hb, w_ref[...], preferred_element_type=F32)
    for j in range(3 * n_heads):
        blk = z[:, j * head_dim:(j + 1) * head_dim]
        if j < n_heads:
            blk = blk * q_scale
        qkv_ref[j] = blk.astype(BF16)

    def log2_forget(f):
        return (jnp.minimum(f, 0.0) - jnp.log1p(jnp.exp(-jnp.abs(f)))) * LOG2E

    row = lax.broadcasted_iota(jnp.int32, (tm, tm), 0)
    col = lax.broadcasted_iota(jnp.int32, (tm, tm), 1)

    fc = jnp.dot(hb, wfc_ref[...], preferred_element_type=F32) + bfc_ref[...]
    lower = jnp.where(col <= row, 1.0, 0.0).astype(BF16)
    cc = jnp.dot(lower, jnp.concatenate(_split3(log2_forget(fc)), axis=1),
                 preferred_element_type=F32)
    cc = cc[:, :LANES] + cc[:, LANES:2 * LANES] + cc[:, 2 * LANES:] + carry_c[...]
    ccol_ref[...] = cc
    carry_c[...] = cc[tm - 1:tm, :]

    r = lax.broadcasted_iota(jnp.int32, (3 * LANES, LANES), 0)
    c = lax.broadcasted_iota(jnp.int32, (3 * LANES, LANES), 1)
    src = lax.shift_right_logical(c, KEY_TERM_SHIFT) * LANES + (c & (KEY_TERM_STRIDE - 1))
    place = jnp.where(c < 3 * KEY_TERM_STRIDE, jnp.where(r == src, 1.0, 0.0), 0.0).astype(BF16)
    kf_ref[...] = jnp.dot(jnp.concatenate(_split3(-cc), axis=1), place,
                          preferred_element_type=F32).astype(BF16)


def _qkv_proj(h2, w_qkv, w_fc, b_fc, l, *, B, S, D, n_heads, head_dim, tm):
    d_att = n_heads * head_dim
    kern = functools.partial(_qkv_kernel, n_heads=n_heads, head_dim=head_dim,
                             q_scale=LOG2E / math.sqrt(head_dim))
    return pl.pallas_call(
        kern,
        grid=(B, S // tm),
        in_specs=[pl.BlockSpec((tm, D), lambda b_, i: (i, b_)),
                  _layer_spec((D, 3 * d_att), l),
                  _layer_spec((D, LANES), l),
                  _layer_spec((1, LANES), l)],
        out_specs=[pl.BlockSpec((3 * n_heads, None, tm, head_dim), lambda b_, i: (0, b_, i, 0)),
                   pl.BlockSpec((None, tm, LANES), lambda b_, i: (b_, i, 0)),
                   pl.BlockSpec((None, tm, LANES), lambda b_, i: (b_, i, 0))],
        out_shape=[jax.ShapeDtypeStruct((3 * n_heads, B, S, head_dim), BF16),
                   jax.ShapeDtypeStruct((B, S, LANES), F32),
                   jax.ShapeDtypeStruct((B, S, LANES), BF16)],
        scratch_shapes=[pltpu.VMEM((1, LANES), F32)],
        compiler_params=_params("parallel", "arbitrary"),
        name="qkv_proj",
    )(h2, w_qkv, w_fc, b_fc)


def _attn_kernel(q_ref, k_ref, v_ref, cq_ref, kf_ref, o_ref, *, n_heads, tq, group):
    i = pl.program_id(1)
    n_q_tiles = kf_ref.shape[0] // tq
    lane = lax.broadcasted_iota(jnp.int32, (tq, LANES), 1)
    r_idx = lax.broadcasted_iota(jnp.int32, (tq, tq), 0)
    c_idx = lax.broadcasted_iota(jnp.int32, (tq, tq), 1)
    causal_bias = jnp.where(c_idx <= r_idx, 0.0, -jnp.inf)
    cq_all = cq_ref[...]
    nt = (((1,), (1,)), ((), ()))

    def chain_inputs(h):
        pick = jnp.where(lane < 3 * KEY_TERM_STRIDE,
                         jnp.where((lane & (KEY_TERM_STRIDE - 1)) == h, 1.0, 0.0), 0.0)
        q_aug = jnp.concatenate([q_ref[h], pick.astype(BF16)], axis=1)
        c_q = jnp.sum(jnp.where(lane == h, cq_all, 0.0), axis=1, keepdims=True)
        return q_aug, c_q

    def tile(h, q_aug, j):
        start = j * tq
        k_aug = jnp.concatenate([k_ref[h, pl.ds(start, tq), :], kf_ref[pl.ds(start, tq), :]], axis=1)
        s = lax.dot_general(q_aug, k_aug, nt, preferred_element_type=F32)
        v_aug = jnp.concatenate([v_ref[h, pl.ds(start, tq), :], jnp.ones((tq, LANES), BF16)], axis=1)
        return s, v_aug

    def head_group(g, _, *, n_full):
        heads = [g * group + u for u in range(group)]
        inputs = [chain_inputs(h) for h in heads]
        state = [None] * group
        for j in range(n_full + 1):
            for u, (h, (q_aug, c_q)) in enumerate(zip(heads, inputs)):
                s, vj = tile(h, q_aug, j)
                if j == n_full:
                    s = s + causal_bias
                row_max = c_q + jnp.max(s, axis=1, keepdims=True)
                if state[u] is None:
                    m = row_max
                    p = jnp.exp2(s + (c_q - m))
                    state[u] = (m, jnp.dot(p.astype(BF16), vj, preferred_element_type=F32))
                else:
                    m, acc = state[u]
                    m_new = jnp.maximum(m, row_max)
                    alpha = jnp.exp2(m - m_new)
                    p = jnp.exp2(s + (c_q - m_new))
                    state[u] = (m_new,
                                alpha * acc + jnp.dot(p.astype(BF16), vj, preferred_element_type=F32))
        for u, h in enumerate(heads):
            acc = state[u][1]
            o_ref[h] = (acc[:, :LANES] / acc[:, LANES:]).astype(o_ref.dtype)
        return 0

    for n_full in range(n_q_tiles):
        @pl.when(i == n_full)
        def _(n_full=n_full):
            lax.fori_loop(0, n_heads // group, functools.partial(head_group, n_full=n_full), 0)


def _attention(qkv, ccol, kf, *, B, S, n_heads, head_dim, tq, group):
    H = n_heads
    kern = functools.partial(_attn_kernel, n_heads=H, tq=tq, group=group)
    return pl.pallas_call(
        kern,
        grid=(B, S // tq),
        in_specs=[pl.BlockSpec((H, None, tq, head_dim), lambda b_, i: (0, b_, i, 0)),
                  pl.BlockSpec((H, None, S, head_dim), lambda b_, i: (1, b_, 0, 0)),
                  pl.BlockSpec((H, None, S, head_dim), lambda b_, i: (2, b_, 0, 0)),
                  pl.BlockSpec((None, tq, LANES), lambda b_, i: (b_, i, 0)),
                  pl.BlockSpec((None, S, LANES), lambda b_, i: (b_, 0, 0))],
        out_specs=pl.BlockSpec((H, None, tq, head_dim), lambda b_, i: (0, b_, i, 0)),
        out_shape=jax.ShapeDtypeStruct((H, B, S, head_dim), BF16),
        compiler_params=_params("parallel", "arbitrary"),
        name="fox_attention",
    )(qkv, qkv, qkv, ccol, kf)


def _row_pitch(tt):
    groups = tt // SUBLANES + 1
    return SUBLANES * (groups if groups % 2 else groups + 1)


def _rnn_kernel(h_ref, w_ref, cw_ref, cb_ref, wax_ref, ba_ref, bx_ref, lam_ref,
                o_ref, rx_bt, x_tb, a_s, u_s, h_bt, h_carry, *, n_batch, n_blocks, blk, pitch):
    i = pl.program_id(0)
    tt = h_ref.shape[0]
    rows, d = a_s.shape
    halo = (CONV_WIDTH - 1) * n_batch

    @pl.when(i == 0)
    def _():
        x_tb[0:halo, :] = jnp.zeros((halo, d), F32)
        h_carry[...] = jnp.zeros_like(h_carry)

    hb = jnp.concatenate([h_ref[:, b * d:(b + 1) * d].astype(BF16) for b in range(n_batch)], axis=0)
    z = jnp.dot(hb, w_ref[...], preferred_element_type=F32)
    for b in range(n_batch):
        for n in range(n_blocks):
            rx_bt[n, b * pitch:b * pitch + tt, :] = z[b * tt:(b + 1) * tt, n * blk:(n + 1) * blk]

    def gather_step(t, _):
        r0 = pl.multiple_of(halo + t * n_batch, n_batch)
        for n in range(n_blocks):
            x_tb[pl.ds(r0, n_batch), n * blk:(n + 1) * blk] = rx_bt[n, pl.ds(t, n_batch, stride=pitch), :]
        return 0

    lax.fori_loop(0, tt, gather_step, 0, unroll=4)

    xc = cb_ref[...] + cw_ref[0:1, :] * x_tb[0:rows, :]
    for k in range(1, CONV_WIDTH):
        xc = xc + cw_ref[k:k + 1, :] * x_tb[k * n_batch:k * n_batch + rows, :]
    x_tb[0:halo, :] = x_tb[rows:rows + halo, :]
    xcb = xc.astype(BF16)

    half_l2 = (-0.5 * RG_C * LOG2E) * (jnp.maximum(-lam_ref[...], 0.0)
                                       + jnp.log1p(jnp.exp(-jnp.abs(lam_ref[...]))))
    half_ba = 0.5 * ba_ref[...]
    half_bx = 0.5 * bx_ref[...]
    for n in range(n_blocks):
        sl = slice(n * blk, (n + 1) * blk)
        g = jnp.dot(xcb[:, sl], wax_ref[n], preferred_element_type=F32)
        t_r = jnp.tanh(g[:, :blk] + half_ba[:, sl])
        t_i = jnp.tanh(g[:, blk:] + half_bx[:, sl])
        a = jnp.exp2(half_l2[:, sl] + half_l2[:, sl] * t_r)
        half_x = 0.5 * xc[:, sl]
        one_m_a2 = 1.0 - a * a
        mult = jnp.where(one_m_a2 > 0.0, one_m_a2 * lax.rsqrt(one_m_a2), 0.0)
        a_s[:, sl] = a
        u_s[:, sl] = mult * (half_x + half_x * t_i)

    def scan_step(t, h):
        r0 = pl.multiple_of(t * n_batch, n_batch)
        h = a_s[pl.ds(r0, n_batch), :] * h + u_s[pl.ds(r0, n_batch), :]
        for n in range(n_blocks):
            h_bt[n, pl.ds(t, n_batch, stride=pitch), :] = h[:, n * blk:(n + 1) * blk]
        return h

    h_carry[...] = lax.fori_loop(0, tt, scan_step, h_carry[...], unroll=4)

    for b in range(n_batch):
        hs = jnp.concatenate([h_bt[n, b * pitch:b * pitch + tt, :] for n in range(n_blocks)], axis=1)
        ry = z[b * tt:(b + 1) * tt, d:]
        gelu = 0.5 * ry * (1.0 + jnp.tanh(math.sqrt(2.0 / math.pi) * (ry + 0.044715 * (ry * ry * ry))))
        o_ref[:, b * d:(b + 1) * d] = (hs * gelu).astype(o_ref.dtype)


def _rnn_branch(h2, w_r, conv_w, conv_b, w_ax, b_a, b_x, lam, l, *, B, S, D, n_blocks, tt):
    rows = tt * B
    blk = D // n_blocks
    pitch = _row_pitch(tt)
    kern = functools.partial(_rnn_kernel, n_batch=B, n_blocks=n_blocks, blk=blk, pitch=pitch)
    return pl.pallas_call(
        kern,
        grid=(S // tt,),
        in_specs=[pl.BlockSpec((tt, B * D), lambda i: (i, 0)),
                  _layer_spec((D, 2 * D), l),
                  _layer_spec((CONV_WIDTH, D), l),
                  _layer_spec((1, D), l),
                  _layer_spec((n_blocks, blk, 2 * blk), l),
                  _layer_spec((1, D), l), _layer_spec((1, D), l), _layer_spec((1, D), l)],
        out_specs=pl.BlockSpec((tt, B * D), lambda i: (i, 0)),
        out_shape=jax.ShapeDtypeStruct((S, B * D), BF16),
        scratch_shapes=[pltpu.VMEM((n_blocks, B * pitch, blk), F32),
                        pltpu.VMEM((rows + (CONV_WIDTH - 1) * B, D), F32),
                        pltpu.VMEM((rows, D), F32), pltpu.VMEM((rows, D), F32),
                        pltpu.VMEM((n_blocks, B * pitch, blk), F32),
                        pltpu.VMEM((B, D), F32)],
        compiler_params=_params("arbitrary"),
        name="rglru_branch",
    )(h2, w_r, conv_w, conv_b, w_ax, b_a, b_x, lam)


def _merge_kernel(h_ref, att_ref, rnn_ref, wg_ref, wa_ref, wr_ref, wo_ref, bm_ref, g_ref, b_ref,
                  o_ref, *, n_heads, alpha):
    h = h_ref[...]
    d = h.shape[1]
    gates = jnp.dot(h.astype(BF16), wg_ref[...], preferred_element_type=F32)
    att = jnp.concatenate([att_ref[j] for j in range(n_heads)], axis=1)
    ya = jnp.dot(att, wa_ref[...], preferred_element_type=F32)
    yb = jnp.dot(rnn_ref[...], wr_ref[...], preferred_element_type=F32)
    merged = (_sigmoid(gates[:, :d] + bm_ref[0:1, :]) * ya
              + _sigmoid(gates[:, d:] + bm_ref[1:2, :]) * yb)
    m = jnp.dot(merged.astype(BF16), wo_ref[...], preferred_element_type=F32)
    o_ref[...] = _layer_norm(alpha * h + m, g_ref[...], b_ref[...])


def _merge(h2, att, rnn2, w_g, w_a, w_r, w_o, b_m, ln_g, ln_b, l, *, B, S, D, n_heads, head_dim,
           ts, alpha):
    kern = functools.partial(_merge_kernel, n_heads=n_heads, alpha=alpha)
    d_att = n_heads * head_dim
    tok = pl.BlockSpec((ts, D), lambda b_, i: (i, b_))
    return pl.pallas_call(
        kern,
        grid=(B, S // ts),
        in_specs=[tok,
                  pl.BlockSpec((n_heads, None, ts, head_dim), lambda b_, i: (0, b_, i, 0)),
                  tok,
                  _layer_spec((D, 2 * D), l), _layer_spec((d_att, D), l),
                  _layer_spec((D, D), l), _layer_spec((D, D), l),
                  _layer_spec((2, D), l), _layer_spec((1, D), l), _layer_spec((1, D), l)],
        out_specs=tok,
        out_shape=jax.ShapeDtypeStruct((S, B * D), F32),
        compiler_params=_params("parallel", "parallel"),
        name="merge_out",
    )(h2, att, rnn2, w_g, w_a, w_r, w_o, b_m, ln_g, ln_b)


def _ffn_kernel(h_ref, p_ref, wi_ref, wo_ref, g1_ref, b1_ref, wpg_ref, bpg_ref, wp_ref,
                g2_ref, b2_ref, o_ref, *, d_ff, chunks, alpha):
    h = h_ref[...]
    hb = h.astype(BF16)
    f = None
    for lo, hi in chunks:
        hg = jnp.dot(hb, wi_ref[:, lo:hi], preferred_element_type=F32)
        hu = jnp.dot(hb, wi_ref[:, d_ff + lo:d_ff + hi], preferred_element_type=F32)
        act = (hg * _sigmoid(hg) * hu).astype(BF16)
        part = jnp.dot(act, wo_ref[lo:hi, :], preferred_element_type=F32)
        f = part if f is None else f + part
    h = _layer_norm(alpha * h + f, g1_ref[...], b1_ref[...])
    gate = _sigmoid(jnp.dot(h.astype(BF16), wpg_ref[...], preferred_element_type=F32) + bpg_ref[...])
    e = gate * jnp.dot(p_ref[...].astype(BF16), wp_ref[...], preferred_element_type=F32)
    o_ref[...] = _layer_norm(alpha * h + e, g2_ref[...], b2_ref[...])


def _ffn_chunks(d_ff, width):
    edges = list(range(0, d_ff, width)) + [d_ff]
    return tuple(zip(edges[:-1], edges[1:]))


def _ffn_ple(h2, p, w_i, w_o, g1, b1, w_pg, b_pg, w_p, g2, b2, l, *, B, S, D, d_ff, d_ple, ts,
             alpha, batch_major_out):
    kern = functools.partial(_ffn_kernel, d_ff=d_ff, chunks=_ffn_chunks(d_ff, 1024), alpha=alpha)
    tok = pl.BlockSpec((ts, D), lambda b_, i: (i, b_))
    if batch_major_out:
        out_spec = pl.BlockSpec((None, ts, D), lambda b_, i: (b_, i, 0))
        out_shape = jax.ShapeDtypeStruct((B, S, D), F32)
    else:
        out_spec, out_shape = tok, jax.ShapeDtypeStruct((S, B * D), F32)
    vec = _layer_spec((1, D), l)
    return pl.pallas_call(
        kern,
        grid=(B, S // ts),
        in_specs=[tok,
                  pl.BlockSpec((None, None, ts, d_ple), lambda b_, i: (l, b_, i, 0)),
                  _layer_spec((D, 2 * d_ff), l), _layer_spec((d_ff, D), l), vec, vec,
                  _layer_spec((D, D), l), vec, _layer_spec((d_ple, D), l), vec, vec],
        out_specs=out_spec,
        out_shape=out_shape,
        compiler_params=_params("parallel", "parallel"),
        name="ffn_ple",
    )(h2, p, w_i, w_o, g1, b1, w_pg, b_pg, w_p, g2, b2)


def kernel(x, p, ln_in_g, ln_in_b, w_in, b_forget, conv_w, conv_b, rg_w_a, rg_b_a, rg_w_x, rg_b_x,
           rg_lambda, w_branch_att, w_branch_rnn, b_merge, w_out, ln_mix_g, ln_mix_b, w_ffn_in,
           w_ffn_out, ln_ffn_g, ln_ffn_b, w_ple, w_ple_gate, b_ple_gate, ln_ple_g, ln_ple_b):
    B, S, D = x.shape
    L = w_in.shape[0]
    H = b_forget.shape[1]
    d_att = w_branch_att.shape[1]
    head_dim = d_att // H
    n_blocks, blk = rg_w_a.shape[1], rg_w_a.shape[2]
    d_ff = w_ffn_out.shape[1]
    d_ple = w_ple.shape[1]
    assert head_dim == LANES and blk == LANES and D == n_blocks * blk and d_att == D
    assert H <= KEY_TERM_STRIDE and H % 2 == 0 and B % 8 == 0
    alpha = float((2 * L) ** 0.25)

    ts = min(512, S)
    tq = min(512, S)
    tt = max(1, min(S, 512 // B))

    o_f, o_rx = 3 * d_att, 3 * d_att + H
    o_g = o_rx + 2 * D
    w_qkv = w_in[:, :, :o_f].astype(BF16)
    w_f = w_in[:, :, o_f:o_rx]
    w_fc = jnp.pad(w_f, ((0, 0), (0, 0), (0, LANES - H))).astype(BF16)
    b_fc = jnp.pad(b_forget, ((0, 0), (0, LANES - H))).reshape(L, 1, LANES)
    w_r = w_in[:, :, o_rx:o_g].astype(BF16)
    w_g = w_in[:, :, o_g:].astype(BF16)
    w_ax = (0.5 * jnp.concatenate([rg_w_a, rg_w_x], axis=-1)).astype(BF16)
    w_ba, w_br, w_o = (w.astype(BF16) for w in (w_branch_att, w_branch_rnn, w_out))
    w_fi, w_fo = w_ffn_in.astype(BF16), w_ffn_out.astype(BF16)
    w_pg, w_p = w_ple_gate.astype(BF16), w_ple.astype(BF16)
    vec = lambda a: a.reshape(L, 1, D)

    h2 = _ln_in(x, ln_in_g, ln_in_b, ts)
    for l in range(L):
        qkv, ccol, kf = _qkv_proj(h2, w_qkv, w_fc, b_fc, l, B=B, S=S, D=D,
                                  n_heads=H, head_dim=head_dim, tm=ts)
        att = _attention(qkv, ccol, kf, B=B, S=S, n_heads=H, head_dim=head_dim, tq=tq, group=2)
        rnn = _rnn_branch(h2, w_r, conv_w, vec(conv_b), w_ax, vec(rg_b_a),
                          vec(rg_b_x), vec(rg_lambda), l, B=B, S=S, D=D, n_blocks=n_blocks, tt=tt)
        h2 = _merge(h2, att, rnn, w_g, w_ba, w_br, w_o, b_merge,
                    vec(ln_mix_g), vec(ln_mix_b), l, B=B, S=S, D=D, n_heads=H, head_dim=head_dim,
                    ts=ts, alpha=alpha)
        h2 = _ffn_ple(h2, p, w_fi, w_fo, vec(ln_ffn_g), vec(ln_ffn_b), w_pg, vec(b_ple_gate), w_p,
                      vec(ln_ple_g), vec(ln_ple_b), l, B=B, S=S, D=D, d_ff=d_ff, d_ple=d_ple,
                      ts=ts, alpha=alpha, batch_major_out=(l == L - 1))
    return h2
```

```python
import functools
import math

import jax
import jax.numpy as jnp
from jax import lax
from jax.experimental import pallas as pl
from jax.experimental.pallas import tpu as pltpu

LN_EPS = 1e-5
RG_C = 8.0
CONV_WIDTH = 4
LOG2E = 1.4426950408889634
LANES = 128
SUBLANES = 8
KEY_TERM_SHIFT = 3
KEY_TERM_STRIDE = 1 << KEY_TERM_SHIFT
VMEM_LIMIT_BYTES = 56 * 1024 * 1024

F32 = jnp.float32
BF16 = jnp.bfloat16


def _params(*semantics):
    return pltpu.CompilerParams(dimension_semantics=semantics,
                                vmem_limit_bytes=VMEM_LIMIT_BYTES)


def _layer_spec(tail, l):
    zeros = (0,) * len(tail)
    return pl.BlockSpec((None,) + tuple(tail), lambda *_: (l,) + zeros,
                        pipeline_mode=pl.Buffered(1))


def _const_spec(shape):
    zeros = (0,) * len(shape)
    return pl.BlockSpec(tuple(shape), lambda *_: zeros, pipeline_mode=pl.Buffered(1))


def _sigmoid(x):
    return 1.0 / (1.0 + jnp.exp(-x))


def _layer_norm(x, g, b):
    mu = jnp.mean(x, axis=-1, keepdims=True)
    xc = x - mu
    var = jnp.mean(xc * xc, axis=-1, keepdims=True)
    return xc * lax.rsqrt(var + LN_EPS) * g + b


def _split3(x):
    hi = x.astype(BF16)
    r1 = x - hi.astype(F32)
    mid = r1.astype(BF16)
    lo = (r1 - mid.astype(F32)).astype(BF16)
    return hi, mid, lo


def _ln_in_kernel(x_ref, g_ref, b_ref, o_ref):
    o_ref[...] = _layer_norm(x_ref[...], g_ref[...], b_ref[...])


def _ln_in(x, g, b, ts):
    B, S, D = x.shape
    return pl.pallas_call(
        _ln_in_kernel,
        grid=(B, S // ts),
        in_specs=[pl.BlockSpec((None, ts, D), lambda b_, i: (b_, i, 0)),
                  _const_spec((1, D)), _const_spec((1, D))],
        out_specs=pl.BlockSpec((ts, D), lambda b_, i: (i, b_)),
        out_shape=jax.ShapeDtypeStruct((S, B * D), F32),
        compiler_params=_params("parallel", "parallel"),
        name="ln_in",
    )(x, g.reshape(1, D), b.reshape(1, D))


def _qkv_kernel(h_ref, w_ref, wfc_ref, bfc_ref, qkv_ref, ccol_ref, kf_ref, carry_c,
                *, n_heads, head_dim, q_scale):
    i = pl.program_id(1)

    @pl.when(i == 0)
    def _():
        carry_c[...] = jnp.zeros_like(carry_c)

    hb = h_ref[...].astype(BF16)
    tm = hb.shape[0]
    z = jnp.dot(hb, w_ref[...], preferred_element_type=F32)
    for j in range(3 * n_heads):
        blk = z[:, j * head_dim:(j + 1) * head_dim]
        if j < n_heads:
            blk = blk * q_scale
        qkv_ref[j] = blk.astype(BF16)

    def log2_forget(f):
        return (jnp.minimum(f, 0.0) - jnp.log1p(jnp.exp(-jnp.abs(f)))) * LOG2E

    row = lax.broadcasted_iota(jnp.int32, (tm, tm), 0)
    col = lax.broadcasted_iota(jnp.int32, (tm, tm), 1)

    fc = jnp.dot(hb, wfc_ref[...], preferred_element_type=F32) + bfc_ref[...]
    lower = jnp.where(col <= row, 1.0, 0.0).astype(BF16)
    cc = jnp.dot(lower, jnp.concatenate(_split3(log2_forget(fc)), axis=1),
                 preferred_element_type=F32)
    cc = cc[:, :LANES] + cc[:, LANES:2 * LANES] + cc[:, 2 * LANES:] + carry_c[...]
    ccol_ref[...] = cc
    carry_c[...] = cc[tm - 1:tm, :]

    r = lax.broadcasted_iota(jnp.int32, (3 * LANES, LANES), 0)
    c = lax.broadcasted_iota(jnp.int32, (3 * LANES, LANES), 1)
    src = lax.shift_right_logical(c, KEY_TERM_SHIFT) * LANES + (c & (KEY_TERM_STRIDE - 1))
    place = jnp.where(c < 3 * KEY_TERM_STRIDE, jnp.where(r == src, 1.0, 0.0), 0.0).astype(BF16)
    kf_ref[...] = jnp.dot(jnp.concatenate(_split3(-cc), axis=1), place,
                          preferred_element_type=F32).astype(BF16)


def _qkv_proj(h2, w_qkv, w_fc, b_fc, l, *, B, S, D, n_heads, head_dim, tm):
    d_att = n_heads * head_dim
    kern = functools.partial(_qkv_kernel, n_heads=n_heads, head_dim=head_dim,
                             q_scale=LOG2E / math.sqrt(head_dim))
    return pl.pallas_call(
        kern,
        grid=(B, S // tm),
        in_specs=[pl.BlockSpec((tm, D), lambda b_, i: (i, b_)),
                  _layer_spec((D, 3 * d_att), l),
                  _layer_spec((D, LANES), l),
                  _layer_spec((1, LANES), l)],
        out_specs=[pl.BlockSpec((3 * n_heads, None, tm, head_dim), lambda b_, i: (0, b_, i, 0)),
                   pl.BlockSpec((None, tm, LANES), lambda b_, i: (b_, i, 0)),
                   pl.BlockSpec((None, tm, LANES), lambda b_, i: (b_, i, 0))],
        out_shape=[jax.ShapeDtypeStruct((3 * n_heads, B, S, head_dim), BF16),
                   jax.ShapeDtypeStruct((B, S, LANES), F32),
                   jax.ShapeDtypeStruct((B, S, LANES), BF16)],
        scratch_shapes=[pltpu.VMEM((1, LANES), F32)],
        compiler_params=_params("parallel", "arbitrary"),
        name="qkv_proj",
    )(h2, w_qkv, w_fc, b_fc)


def _attn_kernel(q_ref, k_ref, v_ref, cq_ref, kf_ref, o_ref, *, n_heads, tq, group):
    i = pl.program_id(1)
    n_q_tiles = kf_ref.shape[0] // tq
    lane = lax.broadcasted_iota(jnp.int32, (tq, LANES), 1)
    r_idx = lax.broadcasted_iota(jnp.int32, (tq, tq), 0)
    c_idx = lax.broadcasted_iota(jnp.int32, (tq, tq), 1)
    causal_bias = jnp.where(c_idx <= r_idx, 0.0, -jnp.inf)
    cq_all = cq_ref[...]
    nt = (((1,), (1,)), ((), ()))

    def chain_inputs(h):
        pick = jnp.where(lane < 3 * KEY_TERM_STRIDE,
                         jnp.where((lane & (KEY_TERM_STRIDE - 1)) == h, 1.0, 0.0), 0.0)
        q_aug = jnp.concatenate([q_ref[h], pick.astype(BF16)], axis=1)
        c_q = jnp.sum(jnp.where(lane == h, cq_all, 0.0), axis=1, keepdims=True)
        return q_aug, c_q

    def tile(h, q_aug, j):
        start = j * tq
        k_aug = jnp.concatenate([k_ref[h, pl.ds(start, tq), :], kf_ref[pl.ds(start, tq), :]], axis=1)
        s = lax.dot_general(q_aug, k_aug, nt, preferred_element_type=F32)
        v_aug = jnp.concatenate([v_ref[h, pl.ds(start, tq), :], jnp.ones((tq, LANES), BF16)], axis=1)
        return s, v_aug

    def head_group(g, _, *, n_full):
        heads = [g * group + u for u in range(group)]
        inputs = [chain_inputs(h) for h in heads]
        state = [None] * group
        for j in range(n_full + 1):
            for u, (h, (q_aug, c_q)) in enumerate(zip(heads, inputs)):
                s, vj = tile(h, q_aug, j)
                if j == n_full:
                    s = s + causal_bias
                row_max = c_q + jnp.max(s, axis=1, keepdims=True)
                if state[u] is None:
                    m = row_max
                    p = jnp.exp2(s + (c_q - m))
                    state[u] = (m, jnp.dot(p.astype(BF16), vj, preferred_element_type=F32))
                else:
                    m, acc = state[u]
                    m_new = jnp.maximum(m, row_max)
                    alpha = jnp.exp2(m - m_new)
                    p = jnp.exp2(s + (c_q - m_new))
                    state[u] = (m_new,
                                alpha * acc + jnp.dot(p.astype(BF16), vj, preferred_element_type=F32))
        for u, h in enumerate(heads):
            acc = state[u][1]
            o_ref[h] = (acc[:, :LANES] / acc[:, LANES:]).astype(o_ref.dtype)
        return 0

    for n_full in range(n_q_tiles):
        @pl.when(i == n_full)
        def _(n_full=n_full):
            lax.fori_loop(0, n_heads // group, functools.partial(head_group, n_full=n_full), 0)


def _attention(qkv, ccol, kf, *, B, S, n_heads, head_dim, tq, group):
    H = n_heads
    kern = functools.partial(_attn_kernel, n_heads=H, tq=tq, group=group)
    return pl.pallas_call(
        kern,
        grid=(B, S // tq),
        in_specs=[pl.BlockSpec((H, None, tq, head_dim), lambda b_, i: (0, b_, i, 0)),
                  pl.BlockSpec((H, None, S, head_dim), lambda b_, i: (1, b_, 0, 0)),
                  pl.BlockSpec((H, None, S, head_dim), lambda b_, i: (2, b_, 0, 0)),
                  pl.BlockSpec((None, tq, LANES), lambda b_, i: (b_, i, 0)),
                  pl.BlockSpec((None, S, LANES), lambda b_, i: (b_, 0, 0))],
        out_specs=pl.BlockSpec((H, None, tq, head_dim), lambda b_, i: (0, b_, i, 0)),
        out_shape=jax.ShapeDtypeStruct((H, B, S, head_dim), BF16),
        compiler_params=_params("parallel", "arbitrary"),
        name="fox_attention",
    )(qkv, qkv, qkv, ccol, kf)


def _row_pitch(tt):
    groups = tt // SUBLANES + 1
    return SUBLANES * (groups if groups % 2 else groups + 1)


def _gelu_tanh(x):
    return 0.5 * x * (1.0 + jnp.tanh(math.sqrt(2.0 / math.pi) * (x + 0.044715 * (x * x * x))))


def _rnn_kernel(h_ref, w_ref, cw_ref, cb_ref, wax_ref, ba_ref, bx_ref, lam_ref,
                o_ref, rx_even, rx_odd, gy_even, gy_odd, x_tb, h_bt, h_carry,
                *, n_batch, n_blocks, blk, pitch):
    i = pl.program_id(0)
    tt = h_ref.shape[0]
    rows, d = tt * n_batch, n_blocks * blk
    halo = (CONV_WIDTH - 1) * n_batch
    chunk = 2 * d // n_blocks

    @pl.when(i == 0)
    def _():
        x_tb[0:halo, :] = jnp.zeros((halo, d), F32)
        h_carry[...] = jnp.zeros_like(h_carry)
        rx_odd[...] = jnp.zeros_like(rx_odd)
        gy_odd[...] = jnp.zeros_like(gy_odd)

    def body(rx_cur, gy_cur, rx_prev, gy_prev):
        half_l2 = (-0.5 * RG_C * LOG2E) * (jnp.maximum(-lam_ref[...], 0.0)
                                           + jnp.log1p(jnp.exp(-jnp.abs(lam_ref[...]))))
        half_ba = 0.5 * ba_ref[...]
        half_bx = 0.5 * bx_ref[...]
        hb = jnp.concatenate([h_ref[:, b * d:(b + 1) * d].astype(BF16) for b in range(n_batch)],
                             axis=0)

        def conv_and_gate_logits(n):
            sl = slice(n * blk, (n + 1) * blk)
            for t in range(tt):
                x_tb[halo + t * n_batch:halo + (t + 1) * n_batch, sl] = (
                    rx_prev[n, pl.ds(t, n_batch, stride=pitch), :])
            xc = cb_ref[:, sl] + cw_ref[0:1, sl] * x_tb[0:rows, sl]
            for k in range(1, CONV_WIDTH):
                xc = xc + cw_ref[k:k + 1, sl] * x_tb[k * n_batch:k * n_batch + rows, sl]
            x_tb[0:halo, sl] = x_tb[rows:rows + halo, sl]
            return xc, jnp.dot(xc.astype(BF16), wax_ref[n], preferred_element_type=F32)

        staged = conv_and_gate_logits(0)
        for n in range(n_blocks):
            sl = slice(n * blk, (n + 1) * blk)
            xc, g = staged
            if n + 1 < n_blocks:
                staged = conv_and_gate_logits(n + 1)
            t_r = jnp.tanh(g[:, :blk] + half_ba[:, sl])
            t_i = jnp.tanh(g[:, blk:] + half_bx[:, sl])
            a = jnp.exp2(half_l2[:, sl] + half_l2[:, sl] * t_r)
            half_x = 0.5 * xc
            one_m_a2 = 1.0 - a * a
            mult = jnp.where(one_m_a2 > 0.0, one_m_a2 * lax.rsqrt(one_m_a2), 0.0)
            u = mult * (half_x + half_x * t_i)

            h = h_carry[:, sl]
            for t in range(tt):
                h = a[t * n_batch:(t + 1) * n_batch, :] * h + u[t * n_batch:(t + 1) * n_batch, :]
                h_bt[n, pl.ds(t, n_batch, stride=pitch), :] = h
            h_carry[:, sl] = jnp.where(i > 0, h, 0.0)

            for b in range(n_batch):
                o_ref[:, b * d + n * blk:b * d + (n + 1) * blk] = (
                    h_bt[n, b * pitch:b * pitch + tt, :] * gy_prev[b * tt:(b + 1) * tt, sl]
                ).astype(o_ref.dtype)

            zc = jnp.dot(hb, w_ref[:, n * chunk:(n + 1) * chunk], preferred_element_type=F32)
            for sub in range(chunk // blk):
                col = n * chunk + sub * blk
                piece = zc[:, sub * blk:(sub + 1) * blk]
                if col < d:
                    for b in range(n_batch):
                        rx_cur[col // blk, b * pitch:b * pitch + tt, :] = piece[b * tt:(b + 1) * tt, :]
                else:
                    gy_cur[:, col - d:col - d + blk] = _gelu_tanh(piece)

    parity = lax.rem(i, 2)

    @pl.when(parity == 0)
    def _():
        body(rx_even, gy_even, rx_odd, gy_odd)

    @pl.when(parity == 1)
    def _():
        body(rx_odd, gy_odd, rx_even, gy_even)


def _rnn_branch(h2, w_r, conv_w, conv_b, w_ax, b_a, b_x, lam, l, *, B, S, D, n_blocks, tt):
    rows = tt * B
    blk = D // n_blocks
    pitch = _row_pitch(tt)
    n_tiles = S // tt
    kern = functools.partial(_rnn_kernel, n_batch=B, n_blocks=n_blocks, blk=blk, pitch=pitch)
    slabs = pltpu.VMEM((n_blocks, B * pitch, blk), F32)
    return pl.pallas_call(
        kern,
        grid=(n_tiles + 1,),
        in_specs=[pl.BlockSpec((tt, B * D), lambda i: (jnp.minimum(i, n_tiles - 1), 0)),
                  _layer_spec((D, 2 * D), l),
                  _layer_spec((CONV_WIDTH, D), l),
                  _layer_spec((1, D), l),
                  _layer_spec((n_blocks, blk, 2 * blk), l),
                  _layer_spec((1, D), l), _layer_spec((1, D), l), _layer_spec((1, D), l)],
        out_specs=pl.BlockSpec((tt, B * D), lambda i: (jnp.maximum(i - 1, 0), 0)),
        out_shape=jax.ShapeDtypeStruct((S, B * D), BF16),
        scratch_shapes=[slabs, slabs,
                        pltpu.VMEM((rows, D), F32), pltpu.VMEM((rows, D), F32),
                        pltpu.VMEM((rows + (CONV_WIDTH - 1) * B, D), F32),
                        slabs,
                        pltpu.VMEM((B, D), F32)],
        compiler_params=_params("arbitrary"),
        name="rglru_branch",
    )(h2, w_r, conv_w, conv_b, w_ax, b_a, b_x, lam)


def _merge_kernel(h_ref, att_ref, rnn_ref, wg_ref, wa_ref, wr_ref, wo_ref, bm_ref, g_ref, b_ref,
                  o_ref, *, n_heads, alpha):
    h = h_ref[...]
    d = h.shape[1]
    gates = jnp.dot(h.astype(BF16), wg_ref[...], preferred_element_type=F32)
    att = jnp.concatenate([att_ref[j] for j in range(n_heads)], axis=1)
    ya = jnp.dot(att, wa_ref[...], preferred_element_type=F32)
    yb = jnp.dot(rnn_ref[...], wr_ref[...], preferred_element_type=F32)
    merged = (_sigmoid(gates[:, :d] + bm_ref[0:1, :]) * ya
              + _sigmoid(gates[:, d:] + bm_ref[1:2, :]) * yb)
    m = jnp.dot(merged.astype(BF16), wo_ref[...], preferred_element_type=F32)
    o_ref[...] = _layer_norm(alpha * h + m, g_ref[...], b_ref[...])


def _merge(h2, att, rnn2, w_g, w_a, w_r, w_o, b_m, ln_g, ln_b, l, *, B, S, D, n_heads, head_dim,
           ts, alpha):
    kern = functools.partial(_merge_kernel, n_heads=n_heads, alpha=alpha)
    d_att = n_heads * head_dim
    tok = pl.BlockSpec((ts, D), lambda b_, i: (i, b_))
    return pl.pallas_call(
        kern,
        grid=(B, S // ts),
        in_specs=[tok,
                  pl.BlockSpec((n_heads, None, ts, head_dim), lambda b_, i: (0, b_, i, 0)),
                  tok,
                  _layer_spec((D, 2 * D), l), _layer_spec((d_att, D), l),
                  _layer_spec((D, D), l), _layer_spec((D, D), l),
                  _layer_spec((2, D), l), _layer_spec((1, D), l), _layer_spec((1, D), l)],
        out_specs=tok,
        out_shape=jax.ShapeDtypeStruct((S, B * D), F32),
        compiler_params=_params("parallel", "parallel"),
        name="merge_out",
    )(h2, att, rnn2, w_g, w_a, w_r, w_o, b_m, ln_g, ln_b)


def _ffn_kernel(h_ref, p_ref, wi_ref, wo_ref, g1_ref, b1_ref, wpg_ref, bpg_ref, wp_ref,
                g2_ref, b2_ref, o_ref, *, d_ff, chunks, alpha):
    h = h_ref[...]
    hb = h.astype(BF16)
    f = None
    for lo, hi in chunks:
        hg = jnp.dot(hb, wi_ref[:, lo:hi], preferred_element_type=F32)
        hu = jnp.dot(hb, wi_ref[:, d_ff + lo:d_ff + hi], preferred_element_type=F32)
        act = (hg * _sigmoid(hg) * hu).astype(BF16)
        part = jnp.dot(act, wo_ref[lo:hi, :], preferred_element_type=F32)
        f = part if f is None else f + part
    h = _layer_norm(alpha * h + f, g1_ref[...], b1_ref[...])
    gate = _sigmoid(jnp.dot(h.astype(BF16), wpg_ref[...], preferred_element_type=F32) + bpg_ref[...])
    e = gate * jnp.dot(p_ref[...].astype(BF16), wp_ref[...], preferred_element_type=F32)
    o_ref[...] = _layer_norm(alpha * h + e, g2_ref[...], b2_ref[...])


def _ffn_chunks(d_ff, width):
    edges = list(range(0, d_ff, width)) + [d_ff]
    return tuple(zip(edges[:-1], edges[1:]))


def _ffn_ple(h2, p, w_i, w_o, g1, b1, w_pg, b_pg, w_p, g2, b2, l, *, B, S, D, d_ff, d_ple, ts,
             alpha, batch_major_out):
    kern = functools.partial(_ffn_kernel, d_ff=d_ff, chunks=_ffn_chunks(d_ff, 1024), alpha=alpha)
    tok = pl.BlockSpec((ts, D), lambda b_, i: (i, b_))
    if batch_major_out:
        out_spec = pl.BlockSpec((None, ts, D), lambda b_, i: (b_, i, 0))
        out_shape = jax.ShapeDtypeStruct((B, S, D), F32)
    else:
        out_spec, out_shape = tok, jax.ShapeDtypeStruct((S, B * D), F32)
    vec = _layer_spec((1, D), l)
    return pl.pallas_call(
        kern,
        grid=(B, S // ts),
        in_specs=[tok,
                  pl.BlockSpec((None, None, ts, d_ple), lambda b_, i: (l, b_, i, 0)),
                  _layer_spec((D, 2 * d_ff), l), _layer_spec((d_ff, D), l), vec, vec,
                  _layer_spec((D, D), l), vec, _layer_spec((d_ple, D), l), vec, vec],
        out_specs=out_spec,
        out_shape=out_shape,
        compiler_params=_params("parallel", "parallel"),
        name="ffn_ple",
    )(h2, p, w_i, w_o, g1, b1, w_pg, b_pg, w_p, g2, b2)


def kernel(x, p, ln_in_g, ln_in_b, w_in, b_forget, conv_w, conv_b, rg_w_a, rg_b_a, rg_w_x, rg_b_x,
           rg_lambda, w_branch_att, w_branch_rnn, b_merge, w_out, ln_mix_g, ln_mix_b, w_ffn_in,
           w_ffn_out, ln_ffn_g, ln_ffn_b, w_ple, w_ple_gate, b_ple_gate, ln_ple_g, ln_ple_b):
    B, S, D = x.shape
    L = w_in.shape[0]
    H = b_forget.shape[1]
    d_att = w_branch_att.shape[1]
    head_dim = d_att // H
    n_blocks, blk = rg_w_a.shape[1], rg_w_a.shape[2]
    d_ff = w_ffn_out.shape[1]
    d_ple = w_ple.shape[1]
    assert head_dim == LANES and blk == LANES and D == n_blocks * blk and d_att == D
    assert H <= KEY_TERM_STRIDE and H % 2 == 0 and B % 8 == 0
    alpha = float((2 * L) ** 0.25)

    ts = min(512, S)
    tq = min(512, S)
    tt = max(1, min(S, 512 // B))

    o_f, o_rx = 3 * d_att, 3 * d_att + H
    o_g = o_rx + 2 * D
    w_qkv = w_in[:, :, :o_f].astype(BF16)
    w_f = w_in[:, :, o_f:o_rx]
    w_fc = jnp.pad(w_f, ((0, 0), (0, 0), (0, LANES - H))).astype(BF16)
    b_fc = jnp.pad(b_forget, ((0, 0), (0, LANES - H))).reshape(L, 1, LANES)
    w_r = w_in[:, :, o_rx:o_g].astype(BF16)
    w_g = w_in[:, :, o_g:].astype(BF16)
    w_ax = (0.5 * jnp.concatenate([rg_w_a, rg_w_x], axis=-1)).astype(BF16)
    w_ba, w_br, w_o = (w.astype(BF16) for w in (w_branch_att, w_branch_rnn, w_out))
    w_fi, w_fo = w_ffn_in.astype(BF16), w_ffn_out.astype(BF16)
    w_pg, w_p = w_ple_gate.astype(BF16), w_ple.astype(BF16)
    vec = lambda a: a.reshape(L, 1, D)

    h2 = _ln_in(x, ln_in_g, ln_in_b, ts)
    for l in range(L):
        qkv, ccol, kf = _qkv_proj(h2, w_qkv, w_fc, b_fc, l, B=B, S=S, D=D,
                                  n_heads=H, head_dim=head_dim, tm=ts)
        att = _attention(qkv, ccol, kf, B=B, S=S, n_heads=H, head_dim=head_dim, tq=tq, group=8)
        rnn = _rnn_branch(h2, w_r, conv_w, vec(conv_b), w_ax, vec(rg_b_a),
                          vec(rg_b_x), vec(rg_lambda), l, B=B, S=S, D=D, n_blocks=n_blocks, tt=tt)
        h2 = _merge(h2, att, rnn, w_g, w_ba, w_br, w_o, b_merge,
                    vec(ln_mix_g), vec(ln_mix_b), l, B=B, S=S, D=D, n_heads=H, head_dim=head_dim,
                    ts=ts, alpha=alpha)
        h2 = _ffn_ple(h2, p, w_fi, w_fo, vec(ln_ffn_g), vec(ln_ffn_b), w_pg, vec(b_ple_gate), w_p,
                      vec(ln_ple_g), vec(ln_ple_b), l, B=B, S=S, D=D, d_ff=d_ff, d_ple=d_ple,
                      ts=ts, alpha=alpha, batch_major_out=(l == L - 1))
    return h2
```

```python
import functools
import math

import jax
import jax.numpy as jnp
from jax import lax
from jax.experimental import pallas as pl
from jax.experimental.pallas import tpu as pltpu

LN_EPS = 1e-5
RG_C = 8.0
CONV_WIDTH = 4
LOG2E = 1.4426950408889634
LANES = 128
SUBLANES = 8
KEY_TERM_SHIFT = 3
KEY_TERM_STRIDE = 1 << KEY_TERM_SHIFT
VMEM_LIMIT_BYTES = 56 * 1024 * 1024

F32 = jnp.float32
BF16 = jnp.bfloat16


def _params(*semantics):
    return pltpu.CompilerParams(dimension_semantics=semantics,
                                vmem_limit_bytes=VMEM_LIMIT_BYTES)


def _layer_spec(tail, l):
    zeros = (0,) * len(tail)
    return pl.BlockSpec((None,) + tuple(tail), lambda *_: (l,) + zeros,
                        pipeline_mode=pl.Buffered(1))


def _const_spec(shape):
    zeros = (0,) * len(shape)
    return pl.BlockSpec(tuple(shape), lambda *_: zeros, pipeline_mode=pl.Buffered(1))


def _sigmoid(x):
    return 1.0 / (1.0 + jnp.exp(-x))


def _layer_norm(x, g, b):
    mu = jnp.mean(x, axis=-1, keepdims=True)
    xc = x - mu
    var = jnp.mean(xc * xc, axis=-1, keepdims=True)
    return xc * lax.rsqrt(var + LN_EPS) * g + b


def _split3(x):
    hi = x.astype(BF16)
    r1 = x - hi.astype(F32)
    mid = r1.astype(BF16)
    lo = (r1 - mid.astype(F32)).astype(BF16)
    return hi, mid, lo


def _ln_in_kernel(x_ref, g_ref, b_ref, o_ref):
    o_ref[...] = _layer_norm(x_ref[...], g_ref[...], b_ref[...])


def _ln_in(x, g, b, ts):
    B, S, D = x.shape
    return pl.pallas_call(
        _ln_in_kernel,
        grid=(B, S // ts),
        in_specs=[pl.BlockSpec((None, ts, D), lambda b_, i: (b_, i, 0)),
                  _const_spec((1, D)), _const_spec((1, D))],
        out_specs=pl.BlockSpec((ts, D), lambda b_, i: (i, b_)),
        out_shape=jax.ShapeDtypeStruct((S, B * D), F32),
        compiler_params=_params("parallel", "parallel"),
        name="ln_in",
    )(x, g.reshape(1, D), b.reshape(1, D))


def _qkv_kernel(h_ref, w_ref, bfc_ref, qkv_ref, ccol_ref, kf_ref, carry_c,
                *, n_heads, head_dim, q_scale):
    i = pl.program_id(1)

    @pl.when(i == 0)
    def _():
        carry_c[...] = jnp.zeros_like(carry_c)

    hb = h_ref[...].astype(BF16)
    tm = hb.shape[0]
    z = jnp.dot(hb, w_ref[...], preferred_element_type=F32)
    for j in range(3 * n_heads):
        blk = z[:, j * head_dim:(j + 1) * head_dim]
        if j < n_heads:
            blk = blk * q_scale
        qkv_ref[j] = blk.astype(BF16)

    def log2_forget(f):
        return (jnp.minimum(f, 0.0) - jnp.log1p(jnp.exp(-jnp.abs(f)))) * LOG2E

    row = lax.broadcasted_iota(jnp.int32, (tm, tm), 0)
    col = lax.broadcasted_iota(jnp.int32, (tm, tm), 1)

    fc = z[:, 3 * n_heads * head_dim:] + bfc_ref[...]
    lower = jnp.where(col <= row, 1.0, 0.0).astype(BF16)
    cc = jnp.dot(lower, jnp.concatenate(_split3(log2_forget(fc)), axis=1),
                 preferred_element_type=F32)
    cc = cc[:, :LANES] + cc[:, LANES:2 * LANES] + cc[:, 2 * LANES:] + carry_c[...]
    ccol_ref[...] = cc
    carry_c[...] = cc[tm - 1:tm, :]

    lane = lax.broadcasted_iota(jnp.int32, (tm, LANES), 1)
    pieces = [jnp.where(lane < n_heads, piece.astype(F32), 0.0) for piece in _split3(-cc)]
    kf = pieces[0]
    for term in (1, 2):
        kf = kf + pltpu.roll(pieces[term], term * KEY_TERM_STRIDE, 1)
    kf_ref[...] = kf.astype(BF16)


def _qkv_proj(h2, w_qkv, b_fc, l, *, B, S, D, n_heads, head_dim, tm):
    d_att = n_heads * head_dim
    kern = functools.partial(_qkv_kernel, n_heads=n_heads, head_dim=head_dim,
                             q_scale=LOG2E / math.sqrt(head_dim))
    return pl.pallas_call(
        kern,
        grid=(B, S // tm),
        in_specs=[pl.BlockSpec((tm, D), lambda b_, i: (i, b_)),
                  _layer_spec((D, 3 * d_att + LANES), l),
                  _layer_spec((1, LANES), l)],
        out_specs=[pl.BlockSpec((3 * n_heads, None, tm, head_dim), lambda b_, i: (0, b_, i, 0)),
                   pl.BlockSpec((None, tm, LANES), lambda b_, i: (b_, i, 0)),
                   pl.BlockSpec((None, tm, LANES), lambda b_, i: (b_, i, 0))],
        out_shape=[jax.ShapeDtypeStruct((3 * n_heads, B, S, head_dim), BF16),
                   jax.ShapeDtypeStruct((B, S, LANES), F32),
                   jax.ShapeDtypeStruct((B, S, LANES), BF16)],
        scratch_shapes=[pltpu.VMEM((1, LANES), F32)],
        compiler_params=_params("parallel", "arbitrary"),
        name="qkv_proj",
    )(h2, w_qkv, b_fc)


def _attn_kernel(q_ref, k_ref, v_ref, cq_ref, kf_ref, o_ref, *, n_heads, tq, group):
    i = pl.program_id(1)
    n_q_tiles = kf_ref.shape[0] // tq
    lane = lax.broadcasted_iota(jnp.int32, (tq, LANES), 1)
    r_idx = lax.broadcasted_iota(jnp.int32, (tq, tq), 0)
    c_idx = lax.broadcasted_iota(jnp.int32, (tq, tq), 1)
    causal_bias = jnp.where(c_idx <= r_idx, 0.0, -jnp.inf)
    cq_all = cq_ref[...]
    nt = (((1,), (1,)), ((), ()))

    def chain_inputs(h):
        pick = jnp.where(lane < 3 * KEY_TERM_STRIDE,
                         jnp.where((lane & (KEY_TERM_STRIDE - 1)) == h, 1.0, 0.0), 0.0)
        q_aug = jnp.concatenate([q_ref[h], pick.astype(BF16)], axis=1)
        c_q = jnp.sum(jnp.where(lane == h, cq_all, 0.0), axis=1, keepdims=True)
        return q_aug, c_q

    def tile(h, q_aug, j):
        start = j * tq
        k_aug = jnp.concatenate([k_ref[h, pl.ds(start, tq), :], kf_ref[pl.ds(start, tq), :]], axis=1)
        s = lax.dot_general(q_aug, k_aug, nt, preferred_element_type=F32)
        v_aug = jnp.concatenate([v_ref[h, pl.ds(start, tq), :], jnp.ones((tq, LANES), BF16)], axis=1)
        return s, v_aug

    def head_group(g, _, *, n_full):
        heads = [g * group + u for u in range(group)]
        inputs = [chain_inputs(h) for h in heads]
        state = [None] * group
        for j in range(n_full + 1):
            for u, (h, (q_aug, c_q)) in enumerate(zip(heads, inputs)):
                s, vj = tile(h, q_aug, j)
                if j == n_full:
                    s = s + causal_bias
                row_max = c_q + jnp.max(s, axis=1, keepdims=True)
                if state[u] is None:
                    m = row_max
                    p = jnp.exp2(s + (c_q - m))
                    state[u] = (m, jnp.dot(p.astype(BF16), vj, preferred_element_type=F32))
                else:
                    m, acc = state[u]
                    m_new = jnp.maximum(m, row_max)
                    alpha = jnp.exp2(m - m_new)
                    p = jnp.exp2(s + (c_q - m_new))
                    state[u] = (m_new,
                                alpha * acc + jnp.dot(p.astype(BF16), vj, preferred_element_type=F32))
        for u, h in enumerate(heads):
            acc = state[u][1]
            o_ref[h] = (acc[:, :LANES] / acc[:, LANES:]).astype(o_ref.dtype)
        return 0

    for n_full in range(n_q_tiles):
        @pl.when(i == n_full)
        def _(n_full=n_full):
            lax.fori_loop(0, n_heads // group, functools.partial(head_group, n_full=n_full), 0)


def _attention(qkv, ccol, kf, *, B, S, n_heads, head_dim, tq, group):
    H = n_heads
    kern = functools.partial(_attn_kernel, n_heads=H, tq=tq, group=group)
    return pl.pallas_call(
        kern,
        grid=(B, S // tq),
        in_specs=[pl.BlockSpec((H, None, tq, head_dim), lambda b_, i: (0, b_, i, 0)),
                  pl.BlockSpec((H, None, S, head_dim), lambda b_, i: (1, b_, 0, 0)),
                  pl.BlockSpec((H, None, S, head_dim), lambda b_, i: (2, b_, 0, 0)),
                  pl.BlockSpec((None, tq, LANES), lambda b_, i: (b_, i, 0)),
                  pl.BlockSpec((None, S, LANES), lambda b_, i: (b_, 0, 0))],
        out_specs=pl.BlockSpec((H, None, tq, head_dim), lambda b_, i: (0, b_, i, 0)),
        out_shape=jax.ShapeDtypeStruct((H, B, S, head_dim), BF16),
        compiler_params=_params("parallel", "arbitrary"),
        name="fox_attention",
    )(qkv, qkv, qkv, ccol, kf)


def _row_pitch(tt):
    groups = tt // SUBLANES + 1
    return SUBLANES * (groups if groups % 2 else groups + 1)


def _gelu_tanh(x):
    return 0.5 * x * (1.0 + jnp.tanh(math.sqrt(2.0 / math.pi) * (x + 0.044715 * (x * x * x))))


def _rnn_kernel(h_ref, w_ref, cw_ref, cb_ref, wax_ref, ba_ref, bx_ref, lam_ref,
                o_ref, rx_even, rx_odd, gy_even, gy_odd, x_tb, h_bt, h_carry,
                *, n_batch, n_blocks, blk, pitch):
    i = pl.program_id(0)
    tt = h_ref.shape[0]
    rows, d = tt * n_batch, n_blocks * blk
    halo = (CONV_WIDTH - 1) * n_batch
    chunk = 2 * d // n_blocks

    @pl.when(i == 0)
    def _():
        x_tb[0:halo, :] = jnp.zeros((halo, d), F32)
        h_carry[...] = jnp.zeros_like(h_carry)
        rx_odd[...] = jnp.zeros_like(rx_odd)
        gy_odd[...] = jnp.zeros_like(gy_odd)

    def body(rx_cur, gy_cur, rx_prev, gy_prev):
        half_l2 = (-0.5 * RG_C * LOG2E) * (jnp.maximum(-lam_ref[...], 0.0)
                                           + jnp.log1p(jnp.exp(-jnp.abs(lam_ref[...]))))
        half_ba = 0.5 * ba_ref[...]
        half_bx = 0.5 * bx_ref[...]
        hb = jnp.concatenate([h_ref[:, b * d:(b + 1) * d].astype(BF16) for b in range(n_batch)],
                             axis=0)

        def conv_and_gate_logits(n):
            sl = slice(n * blk, (n + 1) * blk)
            for t in range(tt):
                x_tb[halo + t * n_batch:halo + (t + 1) * n_batch, sl] = (
                    rx_prev[n, pl.ds(t, n_batch, stride=pitch), :])
            xc = cb_ref[:, sl] + cw_ref[0:1, sl] * x_tb[0:rows, sl]
            for k in range(1, CONV_WIDTH):
                xc = xc + cw_ref[k:k + 1, sl] * x_tb[k * n_batch:k * n_batch + rows, sl]
            x_tb[0:halo, sl] = x_tb[rows:rows + halo, sl]
            return xc, jnp.dot(xc.astype(BF16), wax_ref[n], preferred_element_type=F32)

        staged = conv_and_gate_logits(0)
        for n in range(n_blocks):
            sl = slice(n * blk, (n + 1) * blk)
            xc, g = staged
            if n + 1 < n_blocks:
                staged = conv_and_gate_logits(n + 1)
            t_r = jnp.tanh(g[:, :blk] + half_ba[:, sl])
            t_i = jnp.tanh(g[:, blk:] + half_bx[:, sl])
            a = jnp.exp2(half_l2[:, sl] + half_l2[:, sl] * t_r)
            half_x = 0.5 * xc
            one_m_a2 = 1.0 - a * a
            mult = jnp.where(one_m_a2 > 0.0, one_m_a2 * lax.rsqrt(one_m_a2), 0.0)
            u = mult * (half_x + half_x * t_i)

            h = h_carry[:, sl]
            for t in range(tt):
                h = a[t * n_batch:(t + 1) * n_batch, :] * h + u[t * n_batch:(t + 1) * n_batch, :]
                h_bt[n, pl.ds(t, n_batch, stride=pitch), :] = h
            h_carry[:, sl] = jnp.where(i > 0, h, 0.0)

            for b in range(n_batch):
                o_ref[:, b * d + n * blk:b * d + (n + 1) * blk] = (
                    h_bt[n, b * pitch:b * pitch + tt, :] * gy_prev[b * tt:(b + 1) * tt, sl]
                ).astype(o_ref.dtype)

            zc = jnp.dot(hb, w_ref[:, n * chunk:(n + 1) * chunk], preferred_element_type=F32)
            for sub in range(chunk // blk):
                col = n * chunk + sub * blk
                piece = zc[:, sub * blk:(sub + 1) * blk]
                if col < d:
                    for b in range(n_batch):
                        rx_cur[col // blk, b * pitch:b * pitch + tt, :] = piece[b * tt:(b + 1) * tt, :]
                else:
                    gy_cur[:, col - d:col - d + blk] = _gelu_tanh(piece)

    _by_parity(i, lambda cur, prev: body(*cur, *prev), (rx_even, gy_even), (rx_odd, gy_odd))


def _rnn_branch(h2, w_r, conv_w, conv_b, w_ax, b_a, b_x, lam, l, *, B, S, D, n_blocks, tt):
    rows = tt * B
    blk = D // n_blocks
    pitch = _row_pitch(tt)
    n_tiles = S // tt
    kern = functools.partial(_rnn_kernel, n_batch=B, n_blocks=n_blocks, blk=blk, pitch=pitch)
    slabs = pltpu.VMEM((n_blocks, B * pitch, blk), F32)
    return pl.pallas_call(
        kern,
        grid=(n_tiles + 1,),
        in_specs=[pl.BlockSpec((tt, B * D), lambda i: (jnp.minimum(i, n_tiles - 1), 0)),
                  _layer_spec((D, 2 * D), l),
                  _layer_spec((CONV_WIDTH, D), l),
                  _layer_spec((1, D), l),
                  _layer_spec((n_blocks, blk, 2 * blk), l),
                  _layer_spec((1, D), l), _layer_spec((1, D), l), _layer_spec((1, D), l)],
        out_specs=pl.BlockSpec((tt, B * D), lambda i: (jnp.maximum(i - 1, 0), 0)),
        out_shape=jax.ShapeDtypeStruct((S, B * D), BF16),
        scratch_shapes=[slabs, slabs,
                        pltpu.VMEM((rows, D), F32), pltpu.VMEM((rows, D), F32),
                        pltpu.VMEM((rows + (CONV_WIDTH - 1) * B, D), F32),
                        slabs,
                        pltpu.VMEM((B, D), F32)],
        compiler_params=_params("arbitrary"),
        name="rglru_branch",
    )(h2, w_r, conv_w, conv_b, w_ax, b_a, b_x, lam)


def _by_parity(step, body, even, odd):
    parity = lax.rem(step, 2)

    @pl.when(parity == 0)
    def _():
        body(even, odd)

    @pl.when(parity == 1)
    def _():
        body(odd, even)


def _tile_maps(n_tiles, tiles_per_batch):
    def head(g):
        t = jnp.minimum(g, n_tiles - 1)
        return t % tiles_per_batch, t // tiles_per_batch

    def tail(g):
        t = jnp.maximum(g - 1, 0)
        return t % tiles_per_batch, t // tiles_per_batch
    return head, tail


def _merge_kernel(h_ref, att_ref, rnn_ref, wg_ref, wa_ref, wr_ref, wo_ref, bm_ref, g_ref, b_ref,
                  o_ref, y_even, y_odd, *, n_heads, alpha):
    step = pl.program_id(0)

    @pl.when(step == 0)
    def _():
        y_odd[...] = jnp.zeros_like(y_odd)

    def body(y_cur, y_prev):
        h = h_ref[...]
        d = h.shape[1]
        gates = jnp.dot(h.astype(BF16), wg_ref[...], preferred_element_type=F32)
        o_ref[...] = _layer_norm(y_prev[...], g_ref[...], b_ref[...])
        att = jnp.concatenate([att_ref[j] for j in range(n_heads)], axis=1)
        ya = jnp.dot(att, wa_ref[...], preferred_element_type=F32)
        yb = jnp.dot(rnn_ref[...], wr_ref[...], preferred_element_type=F32)
        merged = (_sigmoid(gates[:, :d] + bm_ref[0:1, :]) * ya
                  + _sigmoid(gates[:, d:] + bm_ref[1:2, :]) * yb)
        m = jnp.dot(merged.astype(BF16), wo_ref[...], preferred_element_type=F32)
        y_cur[...] = alpha * h + m

    _by_parity(step, body, y_even, y_odd)


def _merge(h2, att, rnn2, w_g, w_a, w_r, w_o, b_m, ln_g, ln_b, l, *, B, S, D, n_heads, head_dim,
           ts, alpha):
    kern = functools.partial(_merge_kernel, n_heads=n_heads, alpha=alpha)
    d_att = n_heads * head_dim
    n_tiles = B * (S // ts)
    head, tail = _tile_maps(n_tiles, S // ts)
    tok = pl.BlockSpec((ts, D), lambda g: head(g))
    return pl.pallas_call(
        kern,
        grid=(n_tiles + 1,),
        in_specs=[tok,
                  pl.BlockSpec((n_heads, None, ts, head_dim),
                               lambda g: (0, head(g)[1], head(g)[0], 0)),
                  tok,
                  _layer_spec((D, 2 * D), l), _layer_spec((d_att, D), l),
                  _layer_spec((D, D), l), _layer_spec((D, D), l),
                  _layer_spec((2, D), l), _layer_spec((1, D), l), _layer_spec((1, D), l)],
        out_specs=pl.BlockSpec((ts, D), lambda g: tail(g)),
        out_shape=jax.ShapeDtypeStruct((S, B * D), F32),
        scratch_shapes=[pltpu.VMEM((ts, D), F32), pltpu.VMEM((ts, D), F32)],
        compiler_params=_params("arbitrary"),
        name="merge_out",
    )(h2, att, rnn2, w_g, w_a, w_r, w_o, b_m, ln_g, ln_b)


def _ffn_kernel(h_ref, p_ref, wi_ref, wo_ref, g1_ref, b1_ref, wpg_ref, bpg_ref, wp_ref,
                g2_ref, b2_ref, o_ref, y_even, y_odd, *, d_ff, chunks, alpha):
    step = pl.program_id(0)

    @pl.when(step == 0)
    def _():
        y_odd[...] = jnp.zeros_like(y_odd)

    def body(y_cur, y_prev):
        h = h_ref[...]
        hb = h.astype(BF16)

        def up(lo, hi):
            return (jnp.dot(hb, wi_ref[:, lo:hi], preferred_element_type=F32),
                    jnp.dot(hb, wi_ref[:, d_ff + lo:d_ff + hi], preferred_element_type=F32))

        def down(half_g, hu, lo, hi):
            act = ((half_g + half_g * jnp.tanh(half_g)) * hu).astype(BF16)
            return jnp.dot(act, wo_ref[lo:hi, :], preferred_element_type=F32)

        def finish_previous_tile(h1, gate_logits, emb):
            o_ref[...] = _layer_norm(alpha * h1 + _sigmoid(gate_logits + bpg_ref[...]) * emb,
                                     g2_ref[...], b2_ref[...])

        f = down(*up(*chunks[0]), *chunks[0])
        h1 = _layer_norm(y_prev[...], g1_ref[...], b1_ref[...])
        gate_logits = jnp.dot(h1.astype(BF16), wpg_ref[...], preferred_element_type=F32)
        emb = jnp.dot(p_ref[...].astype(BF16), wp_ref[...], preferred_element_type=F32)
        for lo, hi in chunks[1:2]:
            f = f + down(*up(lo, hi), lo, hi)
        finish_previous_tile(h1, gate_logits, emb)
        for lo, hi in chunks[2:]:
            f = f + down(*up(lo, hi), lo, hi)
        y_cur[...] = alpha * h + f

    _by_parity(step, body, y_even, y_odd)


def _ffn_chunks(d_ff, width):
    edges = list(range(0, d_ff, width)) + [d_ff]
    return tuple(zip(edges[:-1], edges[1:]))


def _ffn_ple(h2, p, w_i, w_o, g1, b1, w_pg, b_pg, w_p, g2, b2, l, *, B, S, D, d_ff, d_ple, ts,
             alpha, batch_major_out):
    kern = functools.partial(_ffn_kernel, d_ff=d_ff, chunks=_ffn_chunks(d_ff, 1024), alpha=alpha)
    n_tiles = B * (S // ts)
    head, tail = _tile_maps(n_tiles, S // ts)
    if batch_major_out:
        out_spec = pl.BlockSpec((None, ts, D), lambda g: (tail(g)[1], tail(g)[0], 0))
        out_shape = jax.ShapeDtypeStruct((B, S, D), F32)
    else:
        out_spec = pl.BlockSpec((ts, D), lambda g: tail(g))
        out_shape = jax.ShapeDtypeStruct((S, B * D), F32)
    vec = _layer_spec((1, D), l)
    return pl.pallas_call(
        kern,
        grid=(n_tiles + 1,),
        in_specs=[pl.BlockSpec((ts, D), lambda g: head(g)),
                  pl.BlockSpec((None, None, ts, d_ple), lambda g: (l, tail(g)[1], tail(g)[0], 0)),
                  _layer_spec((D, 2 * d_ff), l), _layer_spec((d_ff, D), l), vec, vec,
                  _layer_spec((D, D), l), vec, _layer_spec((d_ple, D), l), vec, vec],
        out_specs=out_spec,
        out_shape=out_shape,
        scratch_shapes=[pltpu.VMEM((ts, D), F32), pltpu.VMEM((ts, D), F32)],
        compiler_params=_params("arbitrary"),
        name="ffn_ple",
    )(h2, p, w_i, w_o, g1, b1, w_pg, b_pg, w_p, g2, b2)


def kernel(x, p, ln_in_g, ln_in_b, w_in, b_forget, conv_w, conv_b, rg_w_a, rg_b_a, rg_w_x, rg_b_x,
           rg_lambda, w_branch_att, w_branch_rnn, b_merge, w_out, ln_mix_g, ln_mix_b, w_ffn_in,
           w_ffn_out, ln_ffn_g, ln_ffn_b, w_ple, w_ple_gate, b_ple_gate, ln_ple_g, ln_ple_b):
    B, S, D = x.shape
    L = w_in.shape[0]
    H = b_forget.shape[1]
    d_att = w_branch_att.shape[1]
    head_dim = d_att // H
    n_blocks, blk = rg_w_a.shape[1], rg_w_a.shape[2]
    d_ff = w_ffn_out.shape[1]
    d_ple = w_ple.shape[1]
    assert head_dim == LANES and blk == LANES and D == n_blocks * blk and d_att == D
    assert H <= KEY_TERM_STRIDE and H % 2 == 0 and B % 8 == 0
    alpha = float((2 * L) ** 0.25)

    ts = min(512, S)
    tq = min(512, S)
    tt = max(1, min(S, 512 // B))

    o_f, o_rx = 3 * d_att, 3 * d_att + H
    o_g = o_rx + 2 * D
    w_qkv = jnp.pad(w_in[:, :, :o_rx], ((0, 0), (0, 0), (0, LANES - H))).astype(BF16)
    b_fc = jnp.pad(b_forget, ((0, 0), (0, LANES - H))).reshape(L, 1, LANES)
    w_r = w_in[:, :, o_rx:o_g].astype(BF16)
    w_g = w_in[:, :, o_g:].astype(BF16)
    w_ax = (0.5 * jnp.concatenate([rg_w_a, rg_w_x], axis=-1)).astype(BF16)
    w_ba, w_br, w_o = (w.astype(BF16) for w in (w_branch_att, w_branch_rnn, w_out))
    gate_half = jnp.where(jnp.arange(2 * d_ff) < d_ff, 0.5, 1.0).astype(F32)
    w_fi, w_fo = (w_ffn_in * gate_half).astype(BF16), w_ffn_out.astype(BF16)
    w_pg, w_p = w_ple_gate.astype(BF16), w_ple.astype(BF16)
    vec = lambda a: a.reshape(L, 1, D)

    h2 = _ln_in(x, ln_in_g, ln_in_b, ts)
    for l in range(L):
        qkv, ccol, kf = _qkv_proj(h2, w_qkv, b_fc, l, B=B, S=S, D=D,
                                  n_heads=H, head_dim=head_dim, tm=ts)
        att = _attention(qkv, ccol, kf, B=B, S=S, n_heads=H, head_dim=head_dim, tq=tq, group=H)
        rnn = _rnn_branch(h2, w_r, conv_w, vec(conv_b), w_ax, vec(rg_b_a),
                          vec(rg_b_x), vec(rg_lambda), l, B=B, S=S, D=D, n_blocks=n_blocks, tt=tt)
        h2 = _merge(h2, att, rnn, w_g, w_ba, w_br, w_o, b_merge,
                    vec(ln_mix_g), vec(ln_mix_b), l, B=B, S=S, D=D, n_heads=H, head_dim=head_dim,
                    ts=ts, alpha=alpha)
        h2 = _ffn_ple(h2, p, w_fi, w_fo, vec(ln_ffn_g), vec(ln_ffn_b), w_pg, vec(b_ple_gate), w_p,
                      vec(ln_ple_g), vec(ln_ple_b), l, B=B, S=S, D=D, d_ff=d_ff, d_ple=d_ple,
                      ts=ts, alpha=alpha, batch_major_out=(l == L - 1))
    return h2
```

```python
import functools
import math

import jax
import jax.numpy as jnp
from jax import lax
from jax.experimental import pallas as pl
from jax.experimental.pallas import tpu as pltpu

LN_EPS = 1e-5
RG_C = 8.0
CONV_WIDTH = 4
LOG2E = 1.4426950408889634
SQRT_ARG_FLOOR = 1e-30
LANES = 128
SUBLANES = 8
KEY_TERM_SHIFT = 3
KEY_TERM_STRIDE = 1 << KEY_TERM_SHIFT
VMEM_LIMIT_BYTES = 56 * 1024 * 1024

F32 = jnp.float32
BF16 = jnp.bfloat16


def _params(*semantics):
    return pltpu.CompilerParams(dimension_semantics=semantics,
                                vmem_limit_bytes=VMEM_LIMIT_BYTES)


def _layer_spec(tail, l):
    zeros = (0,) * len(tail)
    return pl.BlockSpec((None,) + tuple(tail), lambda *_: (l,) + zeros,
                        pipeline_mode=pl.Buffered(1))


def _const_spec(shape):
    zeros = (0,) * len(shape)
    return pl.BlockSpec(tuple(shape), lambda *_: zeros, pipeline_mode=pl.Buffered(1))


def _sigmoid(x):
    return 1.0 / (1.0 + jnp.exp(-x))


def _layer_norm(x, g, b):
    mu = jnp.mean(x, axis=-1, keepdims=True)
    xc = x - mu
    var = jnp.mean(xc * xc, axis=-1, keepdims=True)
    return xc * lax.rsqrt(var + LN_EPS) * g + b


def _split3(x):
    hi = x.astype(BF16)
    r1 = x - hi.astype(F32)
    mid = r1.astype(BF16)
    lo = (r1 - mid.astype(F32)).astype(BF16)
    return hi, mid, lo


def _ln_in_kernel(x_ref, g_ref, b_ref, o_ref):
    o_ref[...] = _layer_norm(x_ref[...], g_ref[...], b_ref[...])


def _ln_in(x, g, b, ts):
    B, S, D = x.shape
    return pl.pallas_call(
        _ln_in_kernel,
        grid=(B, S // ts),
        in_specs=[pl.BlockSpec((None, ts, D), lambda b_, i: (b_, i, 0)),
                  _const_spec((1, D)), _const_spec((1, D))],
        out_specs=pl.BlockSpec((ts, D), lambda b_, i: (i, b_)),
        out_shape=jax.ShapeDtypeStruct((S, B * D), F32),
        compiler_params=_params("parallel", "parallel"),
        name="ln_in",
    )(x, g.reshape(1, D), b.reshape(1, D))


def _qkv_kernel(h_ref, w_ref, bfc_ref, qkv_ref, ccol_ref, kf_ref, carry_c,
                *, n_heads, head_dim, q_scale):
    i = pl.program_id(1)

    @pl.when(i == 0)
    def _():
        carry_c[...] = jnp.zeros_like(carry_c)

    hb = h_ref[...].astype(BF16)
    tm = hb.shape[0]
    z = jnp.dot(hb, w_ref[...], preferred_element_type=F32)
    for j in range(3 * n_heads):
        blk = z[:, j * head_dim:(j + 1) * head_dim]
        if j < n_heads:
            blk = blk * q_scale
        qkv_ref[j] = blk.astype(BF16)

    def log2_forget(f):
        return (jnp.minimum(f, 0.0) - jnp.log1p(jnp.exp(-jnp.abs(f)))) * LOG2E

    row = lax.broadcasted_iota(jnp.int32, (tm, tm), 0)
    col = lax.broadcasted_iota(jnp.int32, (tm, tm), 1)

    fc = z[:, 3 * n_heads * head_dim:] + bfc_ref[...]
    lower = jnp.where(col <= row, 1.0, 0.0).astype(BF16)
    cc = jnp.dot(lower, jnp.concatenate(_split3(log2_forget(fc)), axis=1),
                 preferred_element_type=F32)
    cc = cc[:, :LANES] + cc[:, LANES:2 * LANES] + cc[:, 2 * LANES:] + carry_c[...]
    ccol_ref[...] = cc
    carry_c[...] = cc[tm - 1:tm, :]

    lane = lax.broadcasted_iota(jnp.int32, (tm, LANES), 1)
    pieces = [jnp.where(lane < n_heads, piece.astype(F32), 0.0) for piece in _split3(-cc)]
    kf = pieces[0]
    for term in (1, 2):
        kf = kf + pltpu.roll(pieces[term], term * KEY_TERM_STRIDE, 1)
    kf_ref[...] = kf.astype(BF16)


def _qkv_proj(h2, w_qkv, b_fc, l, *, B, S, D, n_heads, head_dim, tm):
    d_att = n_heads * head_dim
    kern = functools.partial(_qkv_kernel, n_heads=n_heads, head_dim=head_dim,
                             q_scale=LOG2E / math.sqrt(head_dim))
    return pl.pallas_call(
        kern,
        grid=(B, S // tm),
        in_specs=[pl.BlockSpec((tm, D), lambda b_, i: (i, b_)),
                  _layer_spec((D, 3 * d_att + LANES), l),
                  _layer_spec((1, LANES), l)],
        out_specs=[pl.BlockSpec((3 * n_heads, None, tm, head_dim), lambda b_, i: (0, b_, i, 0)),
                   pl.BlockSpec((None, tm, LANES), lambda b_, i: (b_, i, 0)),
                   pl.BlockSpec((None, tm, LANES), lambda b_, i: (b_, i, 0))],
        out_shape=[jax.ShapeDtypeStruct((3 * n_heads, B, S, head_dim), BF16),
                   jax.ShapeDtypeStruct((B, S, LANES), F32),
                   jax.ShapeDtypeStruct((B, S, LANES), BF16)],
        scratch_shapes=[pltpu.VMEM((1, LANES), F32)],
        compiler_params=_params("parallel", "arbitrary"),
        name="qkv_proj",
    )(h2, w_qkv, b_fc)


def _attn_kernel(q_ref, k_ref, v_ref, cq_ref, kf_ref, o_ref, *, n_heads, tq, group):
    i = pl.program_id(1)
    n_q_tiles = kf_ref.shape[0] // tq
    lane = lax.broadcasted_iota(jnp.int32, (tq, LANES), 1)
    r_idx = lax.broadcasted_iota(jnp.int32, (tq, tq), 0)
    c_idx = lax.broadcasted_iota(jnp.int32, (tq, tq), 1)
    causal_bias = jnp.where(c_idx <= r_idx, 0.0, -jnp.inf)
    cq_all = cq_ref[...]
    nt = (((1,), (1,)), ((), ()))

    def chain_inputs(h):
        pick = jnp.where(lane < 3 * KEY_TERM_STRIDE,
                         jnp.where((lane & (KEY_TERM_STRIDE - 1)) == h, 1.0, 0.0), 0.0)
        q_aug = jnp.concatenate([q_ref[h], pick.astype(BF16)], axis=1)
        c_q = jnp.sum(jnp.where(lane == h, cq_all, 0.0), axis=1, keepdims=True)
        return q_aug, c_q

    def operands(h, j):
        start = j * tq
        k_aug = jnp.concatenate([k_ref[h, pl.ds(start, tq), :], kf_ref[pl.ds(start, tq), :]], axis=1)
        v_aug = jnp.concatenate([v_ref[h, pl.ds(start, tq), :], jnp.ones((tq, LANES), BF16)], axis=1)
        return k_aug, v_aug

    def advance(state, s, v_aug, c_q):
        row_max = c_q + jnp.max(s, axis=1, keepdims=True)
        if state is None:
            p = jnp.exp2(s + (c_q - row_max))
            return row_max, jnp.dot(p.astype(BF16), v_aug, preferred_element_type=F32)
        m, acc = state
        m_new = jnp.maximum(m, row_max)
        alpha = jnp.exp2(m - m_new)
        p = jnp.exp2(s + (c_q - m_new))
        return m_new, alpha * acc + jnp.dot(p.astype(BF16), v_aug, preferred_element_type=F32)

    def head_group(g, _, *, n_full):
        heads = [g * group + u for u in range(group)]
        inputs = [chain_inputs(h) for h in heads]
        state = [None] * group
        for j in range(n_full + 1):
            for u, (h, (q_aug, c_q)) in enumerate(zip(heads, inputs)):
                k_aug, v_aug = operands(h, j)
                s = lax.dot_general(q_aug, k_aug, nt, preferred_element_type=F32)
                if j == n_full:
                    s = s + causal_bias
                state[u] = advance(state[u], s, v_aug, c_q)
        for u, h in enumerate(heads):
            acc = state[u][1]
            o_ref[h] = (acc[:, :LANES] / acc[:, LANES:]).astype(o_ref.dtype)
        return 0

    for n_full in range(n_q_tiles):
        @pl.when(i == n_full)
        def _(n_full=n_full):
            lax.fori_loop(0, n_heads // group, functools.partial(head_group, n_full=n_full), 0)


def _attention(qkv, ccol, kf, *, B, S, n_heads, head_dim, tq, group):
    H = n_heads
    kern = functools.partial(_attn_kernel, n_heads=H, tq=tq, group=group)
    return pl.pallas_call(
        kern,
        grid=(B, S // tq),
        in_specs=[pl.BlockSpec((H, None, tq, head_dim), lambda b_, i: (0, b_, i, 0)),
                  pl.BlockSpec((H, None, S, head_dim), lambda b_, i: (1, b_, 0, 0)),
                  pl.BlockSpec((H, None, S, head_dim), lambda b_, i: (2, b_, 0, 0)),
                  pl.BlockSpec((None, tq, LANES), lambda b_, i: (b_, i, 0)),
                  pl.BlockSpec((None, S, LANES), lambda b_, i: (b_, 0, 0))],
        out_specs=pl.BlockSpec((H, None, tq, head_dim), lambda b_, i: (0, b_, i, 0)),
        out_shape=jax.ShapeDtypeStruct((H, B, S, head_dim), BF16),
        compiler_params=_params("parallel", "arbitrary"),
        name="fox_attention",
    )(qkv, qkv, qkv, ccol, kf)


def _row_pitch(tt):
    groups = tt // SUBLANES + 1
    return SUBLANES * (groups if groups % 2 else groups + 1)


def _gelu_tanh(x):
    c = math.sqrt(2.0 / math.pi)
    half_x = 0.5 * x
    return half_x + half_x * jnp.tanh(x * (c + (c * 0.044715) * (x * x)))


def _rnn_kernel(h_ref, w_ref, cw_ref, cb_ref, wax_ref, ba_ref, bx_ref, lam_ref,
                o_ref, rx_even, rx_odd, gy_even, gy_odd, x_tb, h_bt, h_carry,
                *, n_batch, n_blocks, blk, pitch):
    i = pl.program_id(0)
    tt = h_ref.shape[0]
    rows, d = tt * n_batch, n_blocks * blk
    halo = (CONV_WIDTH - 1) * n_batch
    chunk = 2 * d // n_blocks

    @pl.when(i == 0)
    def _():
        x_tb[0:halo, :] = jnp.zeros((halo, d), F32)
        h_carry[...] = jnp.zeros_like(h_carry)
        rx_odd[...] = jnp.zeros_like(rx_odd)
        gy_odd[...] = jnp.zeros_like(gy_odd)

    def body(rx_cur, gy_cur, rx_prev, gy_prev):
        half_l2 = (-0.5 * RG_C * LOG2E) * (jnp.maximum(-lam_ref[...], 0.0)
                                           + jnp.log1p(jnp.exp(-jnp.abs(lam_ref[...]))))
        half_ba = 0.5 * ba_ref[...]
        half_bx = 0.5 * bx_ref[...]
        hb = jnp.concatenate([h_ref[:, b * d:(b + 1) * d].astype(BF16) for b in range(n_batch)],
                             axis=0)

        def conv_and_gate_logits(n):
            sl = slice(n * blk, (n + 1) * blk)
            for t in range(tt):
                x_tb[halo + t * n_batch:halo + (t + 1) * n_batch, sl] = (
                    rx_prev[n, pl.ds(t, n_batch, stride=pitch), :])
            half_xc = 0.5 * cb_ref[:, sl] + (0.5 * cw_ref[0:1, sl]) * x_tb[0:rows, sl]
            for k in range(1, CONV_WIDTH):
                half_xc = half_xc + (0.5 * cw_ref[k:k + 1, sl]) * x_tb[k * n_batch:k * n_batch + rows, sl]
            x_tb[0:halo, sl] = x_tb[rows:rows + halo, sl]
            return half_xc, jnp.dot(half_xc.astype(BF16), wax_ref[n], preferred_element_type=F32)

        staged = conv_and_gate_logits(0)
        for n in range(n_blocks):
            sl = slice(n * blk, (n + 1) * blk)
            half_x, g = staged
            if n + 1 < n_blocks:
                staged = conv_and_gate_logits(n + 1)
            t_r = jnp.tanh(g[:, :blk] + half_ba[:, sl])
            t_i = jnp.tanh(g[:, blk:] + half_bx[:, sl])
            a = jnp.exp2(half_l2[:, sl] + half_l2[:, sl] * t_r)
            one_m_a2 = 1.0 - a * a
            mult = one_m_a2 * lax.rsqrt(jnp.maximum(one_m_a2, SQRT_ARG_FLOOR))
            u = mult * (half_x + half_x * t_i)

            h = h_carry[:, sl]
            for t in range(tt):
                h = a[t * n_batch:(t + 1) * n_batch, :] * h + u[t * n_batch:(t + 1) * n_batch, :]
                h_bt[n, pl.ds(t, n_batch, stride=pitch), :] = h
            h_carry[:, sl] = jnp.where(i > 0, h, 0.0)

            for b in range(n_batch):
                o_ref[:, b * d + n * blk:b * d + (n + 1) * blk] = (
                    h_bt[n, b * pitch:b * pitch + tt, :] * gy_prev[b * tt:(b + 1) * tt, sl]
                ).astype(o_ref.dtype)

            zc = jnp.dot(hb, w_ref[:, n * chunk:(n + 1) * chunk], preferred_element_type=F32)
            for sub in range(chunk // blk):
                col = n * chunk + sub * blk
                piece = zc[:, sub * blk:(sub + 1) * blk]
                if col < d:
                    for b in range(n_batch):
                        rx_cur[col // blk, b * pitch:b * pitch + tt, :] = piece[b * tt:(b + 1) * tt, :]
                else:
                    gy_cur[:, col - d:col - d + blk] = _gelu_tanh(piece)

    _by_parity(i, lambda cur, prev: body(*cur, *prev), (rx_even, gy_even), (rx_odd, gy_odd))


def _rnn_branch(h2, w_r, conv_w, conv_b, w_ax, b_a, b_x, lam, l, *, B, S, D, n_blocks, tt):
    rows = tt * B
    blk = D // n_blocks
    pitch = _row_pitch(tt)
    n_tiles = S // tt
    kern = functools.partial(_rnn_kernel, n_batch=B, n_blocks=n_blocks, blk=blk, pitch=pitch)
    slabs = pltpu.VMEM((n_blocks, B * pitch, blk), F32)
    return pl.pallas_call(
        kern,
        grid=(n_tiles + 1,),
        in_specs=[pl.BlockSpec((tt, B * D), lambda i: (jnp.minimum(i, n_tiles - 1), 0)),
                  _layer_spec((D, 2 * D), l),
                  _layer_spec((CONV_WIDTH, D), l),
                  _layer_spec((1, D), l),
                  _layer_spec((n_blocks, blk, 2 * blk), l),
                  _layer_spec((1, D), l), _layer_spec((1, D), l), _layer_spec((1, D), l)],
        out_specs=pl.BlockSpec((tt, B * D), lambda i: (jnp.maximum(i - 1, 0), 0)),
        out_shape=jax.ShapeDtypeStruct((S, B * D), BF16),
        scratch_shapes=[slabs, slabs,
                        pltpu.VMEM((rows, D), F32), pltpu.VMEM((rows, D), F32),
                        pltpu.VMEM((rows + (CONV_WIDTH - 1) * B, D), F32),
                        slabs,
                        pltpu.VMEM((B, D), F32)],
        compiler_params=_params("arbitrary"),
        name="rglru_branch",
    )(h2, w_r, conv_w, conv_b, w_ax, b_a, b_x, lam)


def _by_parity(step, body, even, odd):
    parity = lax.rem(step, 2)

    @pl.when(parity == 0)
    def _():
        body(even, odd)

    @pl.when(parity == 1)
    def _():
        body(odd, even)


def _tile_maps(n_tiles, tiles_per_batch):
    def head(g):
        t = jnp.minimum(g, n_tiles - 1)
        return t % tiles_per_batch, t // tiles_per_batch

    def tail(g):
        t = jnp.maximum(g - 1, 0)
        return t % tiles_per_batch, t // tiles_per_batch
    return head, tail


def _merge_kernel(h_ref, att_ref, rnn_ref, wg_ref, wa_ref, wr_ref, wo_ref, bm_ref, g_ref, b_ref,
                  o_ref, y_even, y_odd, *, n_heads, alpha):
    step = pl.program_id(0)

    @pl.when(step == 0)
    def _():
        y_odd[...] = jnp.zeros_like(y_odd)

    def body(y_cur, y_prev):
        h = h_ref[...]
        d = h.shape[1]
        gates = jnp.dot(h.astype(BF16), wg_ref[...], preferred_element_type=F32)
        o_ref[...] = _layer_norm(y_prev[...], g_ref[...], b_ref[...])
        att = jnp.concatenate([att_ref[j] for j in range(n_heads)], axis=1)
        ya = jnp.dot(att, wa_ref[...], preferred_element_type=F32)
        yb = jnp.dot(rnn_ref[...], wr_ref[...], preferred_element_type=F32)
        merged = (_sigmoid(gates[:, :d] + bm_ref[0:1, :]) * ya
                  + _sigmoid(gates[:, d:] + bm_ref[1:2, :]) * yb)
        m = jnp.dot(merged.astype(BF16), wo_ref[...], preferred_element_type=F32)
        y_cur[...] = alpha * h + m

    _by_parity(step, body, y_even, y_odd)


def _merge(h2, att, rnn2, w_g, w_a, w_r, w_o, b_m, ln_g, ln_b, l, *, B, S, D, n_heads, head_dim,
           ts, alpha):
    kern = functools.partial(_merge_kernel, n_heads=n_heads, alpha=alpha)
    d_att = n_heads * head_dim
    n_tiles = B * (S // ts)
    head, tail = _tile_maps(n_tiles, S // ts)
    tok = pl.BlockSpec((ts, D), lambda g: head(g))
    return pl.pallas_call(
        kern,
        grid=(n_tiles + 1,),
        in_specs=[tok,
                  pl.BlockSpec((n_heads, None, ts, head_dim),
                               lambda g: (0, head(g)[1], head(g)[0], 0)),
                  tok,
                  _layer_spec((D, 2 * D), l), _layer_spec((d_att, D), l),
                  _layer_spec((D, D), l), _layer_spec((D, D), l),
                  _layer_spec((2, D), l), _layer_spec((1, D), l), _layer_spec((1, D), l)],
        out_specs=pl.BlockSpec((ts, D), lambda g: tail(g)),
        out_shape=jax.ShapeDtypeStruct((S, B * D), F32),
        scratch_shapes=[pltpu.VMEM((ts, D), F32), pltpu.VMEM((ts, D), F32)],
        compiler_params=_params("arbitrary"),
        name="merge_out",
    )(h2, att, rnn2, w_g, w_a, w_r, w_o, b_m, ln_g, ln_b)


def _ffn_kernel(h_ref, p_ref, wi_ref, wo_ref, g1_ref, b1_ref, wpg_ref, bpg_ref, wp_ref,
                g2_ref, b2_ref, o_ref, y_even, y_odd, *, d_ff, chunks, alpha):
    step = pl.program_id(0)

    @pl.when(step == 0)
    def _():
        y_odd[...] = jnp.zeros_like(y_odd)

    def body(y_cur, y_prev):
        h = h_ref[...]
        hb = h.astype(BF16)

        def up(lo, hi):
            return (jnp.dot(hb, wi_ref[:, lo:hi], preferred_element_type=F32),
                    jnp.dot(hb, wi_ref[:, d_ff + lo:d_ff + hi], preferred_element_type=F32))

        def down(half_g, hu, lo, hi):
            act = ((half_g + half_g * jnp.tanh(half_g)) * hu).astype(BF16)
            return jnp.dot(act, wo_ref[lo:hi, :], preferred_element_type=F32)

        def finish_previous_tile(h1, gate_logits, emb):
            o_ref[...] = _layer_norm(alpha * h1 + _sigmoid(gate_logits + bpg_ref[...]) * emb,
                                     g2_ref[...], b2_ref[...])

        f = down(*up(*chunks[0]), *chunks[0])
        h1 = _layer_norm(y_prev[...], g1_ref[...], b1_ref[...])
        gate_logits = jnp.dot(h1.astype(BF16), wpg_ref[...], preferred_element_type=F32)
        emb = jnp.dot(p_ref[...].astype(BF16), wp_ref[...], preferred_element_type=F32)
        for lo, hi in chunks[1:2]:
            f = f + down(*up(lo, hi), lo, hi)
        finish_previous_tile(h1, gate_logits, emb)
        for lo, hi in chunks[2:]:
            f = f + down(*up(lo, hi), lo, hi)
        y_cur[...] = alpha * h + f

    _by_parity(step, body, y_even, y_odd)


def _ffn_chunks(d_ff, width):
    edges = list(range(0, d_ff, width)) + [d_ff]
    return tuple(zip(edges[:-1], edges[1:]))


def _ffn_ple(h2, p, w_i, w_o, g1, b1, w_pg, b_pg, w_p, g2, b2, l, *, B, S, D, d_ff, d_ple, ts,
             alpha, batch_major_out):
    kern = functools.partial(_ffn_kernel, d_ff=d_ff, chunks=_ffn_chunks(d_ff, 1024), alpha=alpha)
    n_tiles = B * (S // ts)
    head, tail = _tile_maps(n_tiles, S // ts)
    if batch_major_out:
        out_spec = pl.BlockSpec((None, ts, D), lambda g: (tail(g)[1], tail(g)[0], 0))
        out_shape = jax.ShapeDtypeStruct((B, S, D), F32)
    else:
        out_spec = pl.BlockSpec((ts, D), lambda g: tail(g))
        out_shape = jax.ShapeDtypeStruct((S, B * D), F32)
    vec = _layer_spec((1, D), l)
    return pl.pallas_call(
        kern,
        grid=(n_tiles + 1,),
        in_specs=[pl.BlockSpec((ts, D), lambda g: head(g)),
                  pl.BlockSpec((None, None, ts, d_ple), lambda g: (l, tail(g)[1], tail(g)[0], 0)),
                  _layer_spec((D, 2 * d_ff), l), _layer_spec((d_ff, D), l), vec, vec,
                  _layer_spec((D, D), l), vec, _layer_spec((d_ple, D), l), vec, vec],
        out_specs=out_spec,
        out_shape=out_shape,
        scratch_shapes=[pltpu.VMEM((ts, D), F32), pltpu.VMEM((ts, D), F32)],
        compiler_params=_params("arbitrary"),
        name="ffn_ple",
    )(h2, p, w_i, w_o, g1, b1, w_pg, b_pg, w_p, g2, b2)


def kernel(x, p, ln_in_g, ln_in_b, w_in, b_forget, conv_w, conv_b, rg_w_a, rg_b_a, rg_w_x, rg_b_x,
           rg_lambda, w_branch_att, w_branch_rnn, b_merge, w_out, ln_mix_g, ln_mix_b, w_ffn_in,
           w_ffn_out, ln_ffn_g, ln_ffn_b, w_ple, w_ple_gate, b_ple_gate, ln_ple_g, ln_ple_b):
    B, S, D = x.shape
    L = w_in.shape[0]
    H = b_forget.shape[1]
    d_att = w_branch_att.shape[1]
    head_dim = d_att // H
    n_blocks, blk = rg_w_a.shape[1], rg_w_a.shape[2]
    d_ff = w_ffn_out.shape[1]
    d_ple = w_ple.shape[1]
    assert head_dim == LANES and blk == LANES and D == n_blocks * blk and d_att == D
    assert H <= KEY_TERM_STRIDE and H % 2 == 0 and B % 8 == 0
    alpha = float((2 * L) ** 0.25)

    ts = min(512, S)
    tq = min(512, S)
    tt = max(1, min(S, 512 // B))

    o_f, o_rx = 3 * d_att, 3 * d_att + H
    o_g = o_rx + 2 * D
    w_qkv = jnp.pad(w_in[:, :, :o_rx], ((0, 0), (0, 0), (0, LANES - H))).astype(BF16)
    b_fc = jnp.pad(b_forget, ((0, 0), (0, LANES - H))).reshape(L, 1, LANES)
    w_r = w_in[:, :, o_rx:o_g].astype(BF16)
    w_g = w_in[:, :, o_g:].astype(BF16)
    w_ax = jnp.concatenate([rg_w_a, rg_w_x], axis=-1).astype(BF16)
    w_ba, w_br, w_o = (w.astype(BF16) for w in (w_branch_att, w_branch_rnn, w_out))
    gate_half = jnp.where(jnp.arange(2 * d_ff) < d_ff, 0.5, 1.0).astype(F32)
    w_fi, w_fo = (w_ffn_in * gate_half).astype(BF16), w_ffn_out.astype(BF16)
    w_pg, w_p = w_ple_gate.astype(BF16), w_ple.astype(BF16)
    vec = lambda a: a.reshape(L, 1, D)

    h2 = _ln_in(x, ln_in_g, ln_in_b, ts)
    for l in range(L):
        qkv, ccol, kf = _qkv_proj(h2, w_qkv, b_fc, l, B=B, S=S, D=D,
                                  n_heads=H, head_dim=head_dim, tm=ts)
        att = _attention(qkv, ccol, kf, B=B, S=S, n_heads=H, head_dim=head_dim, tq=tq, group=H)
        rnn = _rnn_branch(h2, w_r, conv_w, vec(conv_b), w_ax, vec(rg_b_a),
                          vec(rg_b_x), vec(rg_lambda), l, B=B, S=S, D=D, n_blocks=n_blocks, tt=tt)
        h2 = _merge(h2, att, rnn, w_g, w_ba, w_br, w_o, b_merge,
                    vec(ln_mix_g), vec(ln_mix_b), l, B=B, S=S, D=D, n_heads=H, head_dim=head_dim,
                    ts=ts, alpha=alpha)
        h2 = _ffn_ple(h2, p, w_fi, w_fo, vec(ln_ffn_g), vec(ln_ffn_b), w_pg, vec(b_ple_gate), w_p,
                      vec(ln_ple_g), vec(ln_ple_b), l, B=B, S=S, D=D, d_ff=d_ff, d_ple=d_ple,
                      ts=ts, alpha=alpha, batch_major_out=(l == L - 1))
    return h2
```

```python
import functools
import math

import jax
import jax.numpy as jnp
from jax import lax
from jax.experimental import pallas as pl
from jax.experimental.pallas import tpu as pltpu

LN_EPS = 1e-5
RG_C = 8.0
CONV_WIDTH = 4
LOG2E = 1.4426950408889634
SQRT_ARG_FLOOR = 1e-30
LANES = 128
SUBLANES = 8
KEY_TERM_SHIFT = 3
KEY_TERM_STRIDE = 1 << KEY_TERM_SHIFT
VMEM_LIMIT_BYTES = 56 * 1024 * 1024

F32 = jnp.float32
BF16 = jnp.bfloat16


def _params(*semantics):
    return pltpu.CompilerParams(dimension_semantics=semantics,
                                vmem_limit_bytes=VMEM_LIMIT_BYTES)


def _layer_spec(tail, l):
    zeros = (0,) * len(tail)
    return pl.BlockSpec((None,) + tuple(tail), lambda *_: (l,) + zeros,
                        pipeline_mode=pl.Buffered(1))


def _const_spec(shape):
    zeros = (0,) * len(shape)
    return pl.BlockSpec(tuple(shape), lambda *_: zeros, pipeline_mode=pl.Buffered(1))


def _sigmoid(x):
    return 1.0 / (1.0 + jnp.exp(-x))


def _layer_norm(x, g, b):
    mu = jnp.mean(x, axis=-1, keepdims=True)
    xc = x - mu
    var = jnp.mean(xc * xc, axis=-1, keepdims=True)
    return xc * lax.rsqrt(var + LN_EPS) * g + b


def _split3(x):
    hi = x.astype(BF16)
    r1 = x - hi.astype(F32)
    mid = r1.astype(BF16)
    lo = (r1 - mid.astype(F32)).astype(BF16)
    return hi, mid, lo


def _ln_in_kernel(x_ref, g_ref, b_ref, o_ref):
    o_ref[...] = _layer_norm(x_ref[...], g_ref[...], b_ref[...])


def _ln_in(x, g, b, ts):
    B, S, D = x.shape
    return pl.pallas_call(
        _ln_in_kernel,
        grid=(B, S // ts),
        in_specs=[pl.BlockSpec((None, ts, D), lambda b_, i: (b_, i, 0)),
                  _const_spec((1, D)), _const_spec((1, D))],
        out_specs=pl.BlockSpec((ts, D), lambda b_, i: (i, b_)),
        out_shape=jax.ShapeDtypeStruct((S, B * D), F32),
        compiler_params=_params("parallel", "parallel"),
        name="ln_in",
    )(x, g.reshape(1, D), b.reshape(1, D))


def _qkv_kernel(h_ref, w_ref, bfc_ref, qkv_ref, ccol_ref, kf_ref, carry_c,
                *, n_heads, head_dim, q_scale):
    i = pl.program_id(1)

    @pl.when(i == 0)
    def _():
        carry_c[...] = jnp.zeros_like(carry_c)

    hb = h_ref[...].astype(BF16)
    tm = hb.shape[0]
    z = jnp.dot(hb, w_ref[...], preferred_element_type=F32)
    for j in range(3 * n_heads):
        blk = z[:, j * head_dim:(j + 1) * head_dim]
        if j < n_heads:
            blk = blk * q_scale
        qkv_ref[j] = blk.astype(BF16)

    def log2_forget(f):
        return (jnp.minimum(f, 0.0) - jnp.log1p(jnp.exp(-jnp.abs(f)))) * LOG2E

    row = lax.broadcasted_iota(jnp.int32, (tm, tm), 0)
    col = lax.broadcasted_iota(jnp.int32, (tm, tm), 1)

    fc = z[:, 3 * n_heads * head_dim:] + bfc_ref[...]
    lower = jnp.where(col <= row, 1.0, 0.0).astype(BF16)
    cc = jnp.dot(lower, jnp.concatenate(_split3(log2_forget(fc)), axis=1),
                 preferred_element_type=F32)
    cc = cc[:, :LANES] + cc[:, LANES:2 * LANES] + cc[:, 2 * LANES:] + carry_c[...]
    ccol_ref[...] = cc
    carry_c[...] = cc[tm - 1:tm, :]

    lane = lax.broadcasted_iota(jnp.int32, (tm, LANES), 1)
    pieces = [jnp.where(lane < n_heads, piece.astype(F32), 0.0) for piece in _split3(-cc)]
    kf = pieces[0]
    for term in (1, 2):
        kf = kf + pltpu.roll(pieces[term], term * KEY_TERM_STRIDE, 1)
    kf_ref[...] = kf.astype(BF16)


def _qkv_proj(h2, w_qkv, b_fc, l, *, B, S, D, n_heads, head_dim, tm):
    d_att = n_heads * head_dim
    kern = functools.partial(_qkv_kernel, n_heads=n_heads, head_dim=head_dim,
                             q_scale=LOG2E / math.sqrt(head_dim))
    return pl.pallas_call(
        kern,
        grid=(B, S // tm),
        in_specs=[pl.BlockSpec((tm, D), lambda b_, i: (i, b_)),
                  _layer_spec((D, 3 * d_att + LANES), l),
                  _layer_spec((1, LANES), l)],
        out_specs=[pl.BlockSpec((3 * n_heads, None, tm, head_dim), lambda b_, i: (0, b_, i, 0)),
                   pl.BlockSpec((None, tm, LANES), lambda b_, i: (b_, i, 0)),
                   pl.BlockSpec((None, tm, LANES), lambda b_, i: (b_, i, 0))],
        out_shape=[jax.ShapeDtypeStruct((3 * n_heads, B, S, head_dim), BF16),
                   jax.ShapeDtypeStruct((B, S, LANES), F32),
                   jax.ShapeDtypeStruct((B, S, LANES), BF16)],
        scratch_shapes=[pltpu.VMEM((1, LANES), F32)],
        compiler_params=_params("parallel", "arbitrary"),
        name="qkv_proj",
    )(h2, w_qkv, b_fc)


def _attn_kernel(q_ref, k_ref, v_ref, cq_ref, kf_ref, o_ref, *, n_heads, tq, group, wide):
    i = pl.program_id(1)
    n_q_tiles = kf_ref.shape[0] // tq
    lane = lax.broadcasted_iota(jnp.int32, (tq, LANES), 1)
    r_idx = lax.broadcasted_iota(jnp.int32, (tq, tq), 0)
    c_idx = lax.broadcasted_iota(jnp.int32, (tq, tq), 1)
    causal_bias = jnp.where(c_idx <= r_idx, 0.0, -jnp.inf)
    cq_all = cq_ref[...]
    nt = (((1,), (1,)), ((), ()))

    def chain_inputs(h):
        pick = jnp.where(lane < 3 * KEY_TERM_STRIDE,
                         jnp.where((lane & (KEY_TERM_STRIDE - 1)) == h, 1.0, 0.0), 0.0)
        q_aug = jnp.concatenate([q_ref[h], pick.astype(BF16)], axis=1)
        c_q = jnp.sum(jnp.where(lane == h, cq_all, 0.0), axis=1, keepdims=True)
        return q_aug, c_q

    def operands(h, start, width):
        keys = pl.ds(start, width)
        k_aug = jnp.concatenate([k_ref[h, keys, :], kf_ref[keys, :]], axis=1)
        v_aug = jnp.concatenate([v_ref[h, keys, :], jnp.ones((width, LANES), BF16)], axis=1)
        return k_aug, v_aug

    def advance(state, s, v_aug, c_q):
        row_max = c_q + jnp.max(s, axis=1, keepdims=True)
        if state is None:
            p = jnp.exp2(s + (c_q - row_max))
            return row_max, jnp.dot(p.astype(BF16), v_aug, preferred_element_type=F32)
        m, acc = state
        m_new = jnp.maximum(m, row_max)
        alpha = jnp.exp2(m - m_new)
        p = jnp.exp2(s + (c_q - m_new))
        return m_new, alpha * acc + jnp.dot(p.astype(BF16), v_aug, preferred_element_type=F32)

    def head_group(g, _, *, n_full):
        heads = [g * group + u for u in range(group)]
        inputs = [chain_inputs(h) for h in heads]
        state = [None] * group
        spans, j = [], 0
        while j < n_full:
            step = min(wide, n_full - j)
            spans.append((j * tq, step * tq, False))
            j += step
        spans.append((n_full * tq, tq, True))
        for start, width, diagonal in spans:
            for u, (h, (q_aug, c_q)) in enumerate(zip(heads, inputs)):
                k_aug, v_aug = operands(h, start, width)
                s = lax.dot_general(q_aug, k_aug, nt, preferred_element_type=F32)
                if diagonal:
                    s = s + causal_bias
                state[u] = advance(state[u], s, v_aug, c_q)
        for u, h in enumerate(heads):
            acc = state[u][1]
            o_ref[h] = (acc[:, :LANES] / acc[:, LANES:]).astype(o_ref.dtype)
        return 0

    for n_full in range(n_q_tiles):
        @pl.when(i == n_full)
        def _(n_full=n_full):
            lax.fori_loop(0, n_heads // group, functools.partial(head_group, n_full=n_full), 0)


def _attention(qkv, ccol, kf, *, B, S, n_heads, head_dim, tq, group):
    H = n_heads
    kern = functools.partial(_attn_kernel, n_heads=H, tq=tq, group=group, wide=3)
    return pl.pallas_call(
        kern,
        grid=(B, S // tq),
        in_specs=[pl.BlockSpec((H, None, tq, head_dim), lambda b_, i: (0, b_, i, 0)),
                  pl.BlockSpec((H, None, S, head_dim), lambda b_, i: (1, b_, 0, 0)),
                  pl.BlockSpec((H, None, S, head_dim), lambda b_, i: (2, b_, 0, 0)),
                  pl.BlockSpec((None, tq, LANES), lambda b_, i: (b_, i, 0)),
                  pl.BlockSpec((None, S, LANES), lambda b_, i: (b_, 0, 0))],
        out_specs=pl.BlockSpec((H, None, tq, head_dim), lambda b_, i: (0, b_, i, 0)),
        out_shape=jax.ShapeDtypeStruct((H, B, S, head_dim), BF16),
        compiler_params=_params("parallel", "arbitrary"),
        name="fox_attention",
    )(qkv, qkv, qkv, ccol, kf)


def _row_pitch(tt):
    groups = tt // SUBLANES + 1
    return SUBLANES * (groups if groups % 2 else groups + 1)


def _gelu_tanh(x):
    c = math.sqrt(2.0 / math.pi)
    half_x = 0.5 * x
    return half_x + half_x * jnp.tanh(x * (c + (c * 0.044715) * (x * x)))


def _rnn_kernel(h_ref, w_ref, cw_ref, cb_ref, wax_ref, ba_ref, bx_ref, lam_ref,
                o_ref, rx_even, rx_odd, gy_even, gy_odd, x_tb, h_bt, h_carry,
                *, n_batch, n_blocks, blk, pitch):
    i = pl.program_id(0)
    tt = h_ref.shape[0]
    rows, d = tt * n_batch, n_blocks * blk
    halo = (CONV_WIDTH - 1) * n_batch
    chunk = 2 * d // n_blocks

    @pl.when(i == 0)
    def _():
        x_tb[0:halo, :] = jnp.zeros((halo, d), F32)
        h_carry[...] = jnp.zeros_like(h_carry)
        rx_odd[...] = jnp.zeros_like(rx_odd)
        gy_odd[...] = jnp.zeros_like(gy_odd)

    def body(rx_cur, gy_cur, rx_prev, gy_prev):
        half_l2 = (-0.5 * RG_C * LOG2E) * (jnp.maximum(-lam_ref[...], 0.0)
                                           + jnp.log1p(jnp.exp(-jnp.abs(lam_ref[...]))))
        half_ba = 0.5 * ba_ref[...]
        half_bx = 0.5 * bx_ref[...]
        hb = jnp.concatenate([h_ref[:, b * d:(b + 1) * d].astype(BF16) for b in range(n_batch)],
                             axis=0)

        def conv_and_gate_logits(n):
            sl = slice(n * blk, (n + 1) * blk)
            for t in range(tt):
                x_tb[halo + t * n_batch:halo + (t + 1) * n_batch, sl] = (
                    rx_prev[n, pl.ds(t, n_batch, stride=pitch), :])
            half_xc = 0.5 * cb_ref[:, sl] + (0.5 * cw_ref[0:1, sl]) * x_tb[0:rows, sl]
            for k in range(1, CONV_WIDTH):
                half_xc = half_xc + (0.5 * cw_ref[k:k + 1, sl]) * x_tb[k * n_batch:k * n_batch + rows, sl]
            x_tb[0:halo, sl] = x_tb[rows:rows + halo, sl]
            return half_xc, jnp.dot(half_xc.astype(BF16), wax_ref[n], preferred_element_type=F32)

        staged = conv_and_gate_logits(0)
        for n in range(n_blocks):
            sl = slice(n * blk, (n + 1) * blk)
            half_x, g = staged
            if n + 1 < n_blocks:
                staged = conv_and_gate_logits(n + 1)
            t_r = jnp.tanh(g[:, :blk] + half_ba[:, sl])
            t_i = jnp.tanh(g[:, blk:] + half_bx[:, sl])
            a = jnp.exp2(half_l2[:, sl] + half_l2[:, sl] * t_r)
            one_m_a2 = 1.0 - a * a
            mult = one_m_a2 * lax.rsqrt(jnp.maximum(one_m_a2, SQRT_ARG_FLOOR))
            u = mult * (half_x + half_x * t_i)

            h = h_carry[:, sl]
            for t in range(tt):
                h = a[t * n_batch:(t + 1) * n_batch, :] * h + u[t * n_batch:(t + 1) * n_batch, :]
                h_bt[n, pl.ds(t, n_batch, stride=pitch), :] = h
            h_carry[:, sl] = jnp.where(i > 0, h, 0.0)

            for b in range(n_batch):
                o_ref[:, b * d + n * blk:b * d + (n + 1) * blk] = (
                    h_bt[n, b * pitch:b * pitch + tt, :] * gy_prev[b * tt:(b + 1) * tt, sl]
                ).astype(o_ref.dtype)

            zc = jnp.dot(hb, w_ref[:, n * chunk:(n + 1) * chunk], preferred_element_type=F32)
            for sub in range(chunk // blk):
                col = n * chunk + sub * blk
                piece = zc[:, sub * blk:(sub + 1) * blk]
                if col < d:
                    for b in range(n_batch):
                        rx_cur[col // blk, b * pitch:b * pitch + tt, :] = piece[b * tt:(b + 1) * tt, :]
                else:
                    gy_cur[:, col - d:col - d + blk] = _gelu_tanh(piece)

    _by_parity(i, lambda cur, prev: body(*cur, *prev), (rx_even, gy_even), (rx_odd, gy_odd))


def _rnn_branch(h2, w_r, conv_w, conv_b, w_ax, b_a, b_x, lam, l, *, B, S, D, n_blocks, tt):
    rows = tt * B
    blk = D // n_blocks
    pitch = _row_pitch(tt)
    n_tiles = S // tt
    kern = functools.partial(_rnn_kernel, n_batch=B, n_blocks=n_blocks, blk=blk, pitch=pitch)
    slabs = pltpu.VMEM((n_blocks, B * pitch, blk), F32)
    return pl.pallas_call(
        kern,
        grid=(n_tiles + 1,),
        in_specs=[pl.BlockSpec((tt, B * D), lambda i: (jnp.minimum(i, n_tiles - 1), 0)),
                  _layer_spec((D, 2 * D), l),
                  _layer_spec((CONV_WIDTH, D), l),
                  _layer_spec((1, D), l),
                  _layer_spec((n_blocks, blk, 2 * blk), l),
                  _layer_spec((1, D), l), _layer_spec((1, D), l), _layer_spec((1, D), l)],
        out_specs=pl.BlockSpec((tt, B * D), lambda i: (jnp.maximum(i - 1, 0), 0)),
        out_shape=jax.ShapeDtypeStruct((S, B * D), BF16),
        scratch_shapes=[slabs, slabs,
                        pltpu.VMEM((rows, D), F32), pltpu.VMEM((rows, D), F32),
                        pltpu.VMEM((rows + (CONV_WIDTH - 1) * B, D), F32),
                        slabs,
                        pltpu.VMEM((B, D), F32)],
        compiler_params=_params("arbitrary"),
        name="rglru_branch",
    )(h2, w_r, conv_w, conv_b, w_ax, b_a, b_x, lam)


def _by_parity(step, body, even, odd):
    parity = lax.rem(step, 2)

    @pl.when(parity == 0)
    def _():
        body(even, odd)

    @pl.when(parity == 1)
    def _():
        body(odd, even)


def _tile_maps(n_tiles, tiles_per_batch):
    def head(g):
        t = jnp.minimum(g, n_tiles - 1)
        return t % tiles_per_batch, t // tiles_per_batch

    def tail(g):
        t = jnp.maximum(g - 1, 0)
        return t % tiles_per_batch, t // tiles_per_batch
    return head, tail


def _merge_kernel(h_ref, att_ref, rnn_ref, wg_ref, wa_ref, wr_ref, wo_ref, bm_ref, g_ref, b_ref,
                  o_ref, y_even, y_odd, *, n_heads, alpha):
    step = pl.program_id(0)

    @pl.when(step == 0)
    def _():
        y_odd[...] = jnp.zeros_like(y_odd)

    def body(y_cur, y_prev):
        h = h_ref[...]
        d = h.shape[1]
        gates = jnp.dot(h.astype(BF16), wg_ref[...], preferred_element_type=F32)
        o_ref[...] = _layer_norm(y_prev[...], g_ref[...], b_ref[...])
        att = jnp.concatenate([att_ref[j] for j in range(n_heads)], axis=1)
        ya = jnp.dot(att, wa_ref[...], preferred_element_type=F32)
        yb = jnp.dot(rnn_ref[...], wr_ref[...], preferred_element_type=F32)
        merged = (_sigmoid(gates[:, :d] + bm_ref[0:1, :]) * ya
                  + _sigmoid(gates[:, d:] + bm_ref[1:2, :]) * yb)
        m = jnp.dot(merged.astype(BF16), wo_ref[...], preferred_element_type=F32)
        y_cur[...] = alpha * h + m

    _by_parity(step, body, y_even, y_odd)


def _merge(h2, att, rnn2, w_g, w_a, w_r, w_o, b_m, ln_g, ln_b, l, *, B, S, D, n_heads, head_dim,
           ts, alpha):
    kern = functools.partial(_merge_kernel, n_heads=n_heads, alpha=alpha)
    d_att = n_heads * head_dim
    n_tiles = B * (S // ts)
    head, tail = _tile_maps(n_tiles, S // ts)
    tok = pl.BlockSpec((ts, D), lambda g: head(g))
    return pl.pallas_call(
        kern,
        grid=(n_tiles + 1,),
        in_specs=[tok,
                  pl.BlockSpec((n_heads, None, ts, head_dim),
                               lambda g: (0, head(g)[1], head(g)[0], 0)),
                  tok,
                  _layer_spec((D, 2 * D), l), _layer_spec((d_att, D), l),
                  _layer_spec((D, D), l), _layer_spec((D, D), l),
                  _layer_spec((2, D), l), _layer_spec((1, D), l), _layer_spec((1, D), l)],
        out_specs=pl.BlockSpec((ts, D), lambda g: tail(g)),
        out_shape=jax.ShapeDtypeStruct((S, B * D), F32),
        scratch_shapes=[pltpu.VMEM((ts, D), F32), pltpu.VMEM((ts, D), F32)],
        compiler_params=_params("arbitrary"),
        name="merge_out",
    )(h2, att, rnn2, w_g, w_a, w_r, w_o, b_m, ln_g, ln_b)


def _ffn_kernel(h_ref, p_ref, wi_ref, wo_ref, g1_ref, b1_ref, wpg_ref, bpg_ref, wp_ref,
                g2_ref, b2_ref, o_ref, y_even, y_odd, *, d_ff, chunks, alpha):
    step = pl.program_id(0)

    @pl.when(step == 0)
    def _():
        y_odd[...] = jnp.zeros_like(y_odd)

    def body(y_cur, y_prev):
        h = h_ref[...]
        hb = h.astype(BF16)

        def up(lo, hi):
            return (jnp.dot(hb, wi_ref[:, lo:hi], preferred_element_type=F32),
                    jnp.dot(hb, wi_ref[:, d_ff + lo:d_ff + hi], preferred_element_type=F32))

        def down(half_g, hu, lo, hi):
            act = ((half_g + half_g * jnp.tanh(half_g)) * hu).astype(BF16)
            return jnp.dot(act, wo_ref[lo:hi, :], preferred_element_type=F32)

        def finish_previous_tile(h1, gate_logits, emb):
            o_ref[...] = _layer_norm(alpha * h1 + _sigmoid(gate_logits + bpg_ref[...]) * emb,
                                     g2_ref[...], b2_ref[...])

        f = down(*up(*chunks[0]), *chunks[0])
        h1 = _layer_norm(y_prev[...], g1_ref[...], b1_ref[...])
        gate_logits = jnp.dot(h1.astype(BF16), wpg_ref[...], preferred_element_type=F32)
        emb = jnp.dot(p_ref[...].astype(BF16), wp_ref[...], preferred_element_type=F32)
        for lo, hi in chunks[1:2]:
            f = f + down(*up(lo, hi), lo, hi)
        finish_previous_tile(h1, gate_logits, emb)
        for lo, hi in chunks[2:]:
            f = f + down(*up(lo, hi), lo, hi)
        y_cur[...] = alpha * h + f

    _by_parity(step, body, y_even, y_odd)


def _ffn_chunks(d_ff, width):
    edges = list(range(0, d_ff, width)) + [d_ff]
    return tuple(zip(edges[:-1], edges[1:]))


def _ffn_ple(h2, p, w_i, w_o, g1, b1, w_pg, b_pg, w_p, g2, b2, l, *, B, S, D, d_ff, d_ple, ts,
             alpha, batch_major_out):
    kern = functools.partial(_ffn_kernel, d_ff=d_ff, chunks=_ffn_chunks(d_ff, 1024), alpha=alpha)
    n_tiles = B * (S // ts)
    head, tail = _tile_maps(n_tiles, S // ts)
    if batch_major_out:
        out_spec = pl.BlockSpec((None, ts, D), lambda g: (tail(g)[1], tail(g)[0], 0))
        out_shape = jax.ShapeDtypeStruct((B, S, D), F32)
    else:
        out_spec = pl.BlockSpec((ts, D), lambda g: tail(g))
        out_shape = jax.ShapeDtypeStruct((S, B * D), F32)
    vec = _layer_spec((1, D), l)
    return pl.pallas_call(
        kern,
        grid=(n_tiles + 1,),
        in_specs=[pl.BlockSpec((ts, D), lambda g: head(g)),
                  pl.BlockSpec((None, None, ts, d_ple), lambda g: (l, tail(g)[1], tail(g)[0], 0)),
                  _layer_spec((D, 2 * d_ff), l), _layer_spec((d_ff, D), l), vec, vec,
                  _layer_spec((D, D), l), vec, _layer_spec((d_ple, D), l), vec, vec],
        out_specs=out_spec,
        out_shape=out_shape,
        scratch_shapes=[pltpu.VMEM((ts, D), F32), pltpu.VMEM((ts, D), F32)],
        compiler_params=_params("arbitrary"),
        name="ffn_ple",
    )(h2, p, w_i, w_o, g1, b1, w_pg, b_pg, w_p, g2, b2)


def kernel(x, p, ln_in_g, ln_in_b, w_in, b_forget, conv_w, conv_b, rg_w_a, rg_b_a, rg_w_x, rg_b_x,
           rg_lambda, w_branch_att, w_branch_rnn, b_merge, w_out, ln_mix_g, ln_mix_b, w_ffn_in,
           w_ffn_out, ln_ffn_g, ln_ffn_b, w_ple, w_ple_gate, b_ple_gate, ln_ple_g, ln_ple_b):
    B, S, D = x.shape
    L = w_in.shape[0]
    H = b_forget.shape[1]
    d_att = w_branch_att.shape[1]
    head_dim = d_att // H
    n_blocks, blk = rg_w_a.shape[1], rg_w_a.shape[2]
    d_ff = w_ffn_out.shape[1]
    d_ple = w_ple.shape[1]
    assert head_dim == LANES and blk == LANES and D == n_blocks * blk and d_att == D
    assert H <= KEY_TERM_STRIDE and H % 2 == 0 and B % 8 == 0
    alpha = float((2 * L) ** 0.25)

    ts = min(512, S)
    tq = min(512, S)
    tt = max(1, min(S, 512 // B))

    o_f, o_rx = 3 * d_att, 3 * d_att + H
    o_g = o_rx + 2 * D
    w_qkv = jnp.pad(w_in[:, :, :o_rx], ((0, 0), (0, 0), (0, LANES - H))).astype(BF16)
    b_fc = jnp.pad(b_forget, ((0, 0), (0, LANES - H))).reshape(L, 1, LANES)
    w_r = w_in[:, :, o_rx:o_g].astype(BF16)
    w_g = w_in[:, :, o_g:].astype(BF16)
    w_ax = jnp.concatenate([rg_w_a, rg_w_x], axis=-1).astype(BF16)
    w_ba, w_br, w_o = (w.astype(BF16) for w in (w_branch_att, w_branch_rnn, w_out))
    gate_half = jnp.where(jnp.arange(2 * d_ff) < d_ff, 0.5, 1.0).astype(F32)
    w_fi, w_fo = (w_ffn_in * gate_half).astype(BF16), w_ffn_out.astype(BF16)
    w_pg, w_p = w_ple_gate.astype(BF16), w_ple.astype(BF16)
    vec = lambda a: a.reshape(L, 1, D)

    h2 = _ln_in(x, ln_in_g, ln_in_b, min(1024, S))
    for l in range(L):
        qkv, ccol, kf = _qkv_proj(h2, w_qkv, b_fc, l, B=B, S=S, D=D,
                                  n_heads=H, head_dim=head_dim, tm=ts)
        att = _attention(qkv, ccol, kf, B=B, S=S, n_heads=H, head_dim=head_dim, tq=tq, group=H)
        rnn = _rnn_branch(h2, w_r, conv_w, vec(conv_b), w_ax, vec(rg_b_a),
                          vec(rg_b_x), vec(rg_lambda), l, B=B, S=S, D=D, n_blocks=n_blocks, tt=tt)
        h2 = _merge(h2, att, rnn, w_g, w_ba, w_br, w_o, b_merge,
                    vec(ln_mix_g), vec(ln_mix_b), l, B=B, S=S, D=D, n_heads=H, head_dim=head_dim,
                    ts=ts, alpha=alpha)
        h2 = _ffn_ple(h2, p, w_fi, w_fo, vec(ln_ffn_g), vec(ln_ffn_b), w_pg, vec(b_ple_gate), w_p,
                      vec(ln_ple_g), vec(ln_ple_b), l, B=B, S=S, D=D, d_ff=d_ff, d_ple=d_ple,
                      ts=ts, alpha=alpha, batch_major_out=(l == L - 1))
    return h2
```

```python
import functools
import math

import jax
import jax.numpy as jnp
from jax import lax
from jax.experimental import pallas as pl
from jax.experimental.pallas import tpu as pltpu

LN_EPS = 1e-5
RG_C = 8.0
CONV_WIDTH = 4
LOG2E = 1.4426950408889634
SQRT_ARG_FLOOR = 1e-30
LANES = 128
SUBLANES = 8
KEY_TERM_SHIFT = 3
KEY_TERM_STRIDE = 1 << KEY_TERM_SHIFT
VMEM_LIMIT_BYTES = 56 * 1024 * 1024
ROW_TILE = 512
FFN_CHUNK_COLS = 1024

F32 = jnp.float32
BF16 = jnp.bfloat16


def _params(*semantics):
    return pltpu.CompilerParams(dimension_semantics=semantics,
                                vmem_limit_bytes=VMEM_LIMIT_BYTES)


def _layer_spec(tail, l):
    zeros = (0,) * len(tail)
    return pl.BlockSpec((None,) + tuple(tail), lambda *_: (l,) + zeros,
                        pipeline_mode=pl.Buffered(1))


def _const_spec(shape):
    zeros = (0,) * len(shape)
    return pl.BlockSpec(tuple(shape), lambda *_: zeros, pipeline_mode=pl.Buffered(1))


def _sigmoid(x):
    return 1.0 / (1.0 + jnp.exp(-x))


def _layer_norm(x, g, b):
    mu = jnp.mean(x, axis=-1, keepdims=True)
    xc = x - mu
    var = jnp.mean(xc * xc, axis=-1, keepdims=True)
    return xc * lax.rsqrt(var + LN_EPS) * g + b


def _split3(x):
    hi = x.astype(BF16)
    r1 = x - hi.astype(F32)
    mid = r1.astype(BF16)
    lo = (r1 - mid.astype(F32)).astype(BF16)
    return hi, mid, lo


def _ln_in_kernel(x_ref, g_ref, b_ref, o_ref):
    o_ref[...] = _layer_norm(x_ref[...], g_ref[...], b_ref[...])


def _ln_in(x, g, b, ts):
    B, S, D = x.shape
    return pl.pallas_call(
        _ln_in_kernel,
        grid=(B, S // ts),
        in_specs=[pl.BlockSpec((None, ts, D), lambda b_, i: (b_, i, 0)),
                  _const_spec((1, D)), _const_spec((1, D))],
        out_specs=pl.BlockSpec((ts, D), lambda b_, i: (i, b_)),
        out_shape=jax.ShapeDtypeStruct((S, B * D), F32),
        compiler_params=_params("parallel", "parallel"),
        name="ln_in",
    )(x, g.reshape(1, D), b.reshape(1, D))


def _qkv_kernel(h_ref, w_ref, bfc_ref, qkv_ref, ccol_ref, kf_ref, carry_c,
                *, n_heads, head_dim, q_scale):
    i = pl.program_id(1)

    @pl.when(i == 0)
    def _():
        carry_c[...] = jnp.zeros_like(carry_c)

    hb = h_ref[...].astype(BF16)
    tm = hb.shape[0]
    z = jnp.dot(hb, w_ref[...], preferred_element_type=F32)
    for j in range(3 * n_heads):
        blk = z[:, j * head_dim:(j + 1) * head_dim]
        if j < n_heads:
            blk = blk * q_scale
        qkv_ref[j] = blk.astype(BF16)

    def log2_forget(f):
        return (jnp.minimum(f, 0.0) - jnp.log1p(jnp.exp(-jnp.abs(f)))) * LOG2E

    row = lax.broadcasted_iota(jnp.int32, (tm, tm), 0)
    col = lax.broadcasted_iota(jnp.int32, (tm, tm), 1)

    fc = z[:, 3 * n_heads * head_dim:] + bfc_ref[...]
    lower = jnp.where(col <= row, 1.0, 0.0).astype(BF16)
    cc = jnp.dot(lower, jnp.concatenate(_split3(log2_forget(fc)), axis=1),
                 preferred_element_type=F32)
    cc = cc[:, :LANES] + cc[:, LANES:2 * LANES] + cc[:, 2 * LANES:] + carry_c[...]
    ccol_ref[...] = cc
    carry_c[...] = cc[tm - 1:tm, :]

    lane = lax.broadcasted_iota(jnp.int32, (tm, LANES), 1)
    pieces = [jnp.where(lane < n_heads, piece.astype(F32), 0.0) for piece in _split3(-cc)]
    kf = pieces[0]
    for term in (1, 2):
        kf = kf + pltpu.roll(pieces[term], term * KEY_TERM_STRIDE, 1)
    kf_ref[...] = kf.astype(BF16)


def _qkv_proj(h2, w_qkv, b_fc, l, *, B, S, D, n_heads, head_dim, tm):
    d_att = n_heads * head_dim
    kern = functools.partial(_qkv_kernel, n_heads=n_heads, head_dim=head_dim,
                             q_scale=LOG2E / math.sqrt(head_dim))
    return pl.pallas_call(
        kern,
        grid=(B, S // tm),
        in_specs=[pl.BlockSpec((tm, D), lambda b_, i: (i, b_)),
                  _layer_spec((D, 3 * d_att + LANES), l),
                  _layer_spec((1, LANES), l)],
        out_specs=[pl.BlockSpec((3 * n_heads, None, tm, head_dim), lambda b_, i: (0, b_, i, 0)),
                   pl.BlockSpec((None, tm, LANES), lambda b_, i: (b_, i, 0)),
                   pl.BlockSpec((None, tm, LANES), lambda b_, i: (b_, i, 0))],
        out_shape=[jax.ShapeDtypeStruct((3 * n_heads, B, S, head_dim), BF16),
                   jax.ShapeDtypeStruct((B, S, LANES), F32),
                   jax.ShapeDtypeStruct((B, S, LANES), BF16)],
        scratch_shapes=[pltpu.VMEM((1, LANES), F32)],
        compiler_params=_params("parallel", "arbitrary"),
        name="qkv_proj",
    )(h2, w_qkv, b_fc)


def _attn_kernel(q_ref, k_ref, v_ref, cq_ref, kf_ref, o_ref, *, n_heads, tq, group, wide):
    i = pl.program_id(1)
    n_q_tiles = kf_ref.shape[0] // tq
    lane = lax.broadcasted_iota(jnp.int32, (tq, LANES), 1)
    r_idx = lax.broadcasted_iota(jnp.int32, (tq, tq), 0)
    c_idx = lax.broadcasted_iota(jnp.int32, (tq, tq), 1)
    causal_bias = jnp.where(c_idx <= r_idx, 0.0, -jnp.inf)
    cq_all = cq_ref[...]
    nt = (((1,), (1,)), ((), ()))

    def chain_inputs(h):
        pick = jnp.where(lane < 3 * KEY_TERM_STRIDE,
                         jnp.where((lane & (KEY_TERM_STRIDE - 1)) == h, 1.0, 0.0), 0.0)
        q_aug = jnp.concatenate([q_ref[h], pick.astype(BF16)], axis=1)
        c_q = jnp.sum(jnp.where(lane == h, cq_all, 0.0), axis=1, keepdims=True)
        return q_aug, c_q

    def operands(h, start, width):
        keys = pl.ds(start, width)
        k_aug = jnp.concatenate([k_ref[h, keys, :], kf_ref[keys, :]], axis=1)
        v_aug = jnp.concatenate([v_ref[h, keys, :], jnp.ones((width, LANES), BF16)], axis=1)
        return k_aug, v_aug

    def advance(state, s, v_aug, c_q):
        row_max = c_q + jnp.max(s, axis=1, keepdims=True)
        if state is None:
            p = jnp.exp2(s + (c_q - row_max))
            return row_max, jnp.dot(p.astype(BF16), v_aug, preferred_element_type=F32)
        m, acc = state
        m_new = jnp.maximum(m, row_max)
        alpha = jnp.exp2(m - m_new)
        p = jnp.exp2(s + (c_q - m_new))
        return m_new, alpha * acc + jnp.dot(p.astype(BF16), v_aug, preferred_element_type=F32)

    def head_group(g, _, *, n_full):
        heads = [g * group + u for u in range(group)]
        inputs = [chain_inputs(h) for h in heads]
        state = [None] * group
        spans, j = [], 0
        while j < n_full:
            step = min(wide, n_full - j)
            spans.append((j * tq, step * tq, False))
            j += step
        spans.append((n_full * tq, tq, True))
        for start, width, diagonal in spans:
            for u, (h, (q_aug, c_q)) in enumerate(zip(heads, inputs)):
                k_aug, v_aug = operands(h, start, width)
                s = lax.dot_general(q_aug, k_aug, nt, preferred_element_type=F32)
                if diagonal:
                    s = s + causal_bias
                state[u] = advance(state[u], s, v_aug, c_q)
        for u, h in enumerate(heads):
            acc = state[u][1]
            o_ref[h] = (acc[:, :LANES] / acc[:, LANES:]).astype(o_ref.dtype)
        return 0

    for n_full in range(n_q_tiles):
        @pl.when(i == n_full)
        def _(n_full=n_full):
            lax.fori_loop(0, n_heads // group, functools.partial(head_group, n_full=n_full), 0)


def _attention(qkv, ccol, kf, *, B, S, n_heads, head_dim, tq, group):
    H = n_heads
    kern = functools.partial(_attn_kernel, n_heads=H, tq=tq, group=group, wide=3)
    return pl.pallas_call(
        kern,
        grid=(B, S // tq),
        in_specs=[pl.BlockSpec((H, None, tq, head_dim), lambda b_, i: (0, b_, i, 0)),
                  pl.BlockSpec((H, None, S, head_dim), lambda b_, i: (1, b_, 0, 0)),
                  pl.BlockSpec((H, None, S, head_dim), lambda b_, i: (2, b_, 0, 0)),
                  pl.BlockSpec((None, tq, LANES), lambda b_, i: (b_, i, 0)),
                  pl.BlockSpec((None, S, LANES), lambda b_, i: (b_, 0, 0))],
        out_specs=pl.BlockSpec((H, None, tq, head_dim), lambda b_, i: (0, b_, i, 0)),
        out_shape=jax.ShapeDtypeStruct((H, B, S, head_dim), BF16),
        compiler_params=_params("parallel", "arbitrary"),
        name="fox_attention",
    )(qkv, qkv, qkv, ccol, kf)


def _row_pitch(tt):
    groups = tt // SUBLANES + 1
    return SUBLANES * (groups if groups % 2 else groups + 1)


def _gelu_tanh(x):
    c = math.sqrt(2.0 / math.pi)
    half_x = 0.5 * x
    return half_x + half_x * jnp.tanh(x * (c + (c * 0.044715) * (x * x)))


def _rnn_kernel(h_ref, w_ref, cw_ref, cb_ref, wax_ref, ba_ref, bx_ref, lam_ref,
                o_ref, rx_even, rx_odd, gy_even, gy_odd, x_tb, h_bt, h_carry,
                *, n_batch, n_blocks, blk, pitch):
    i = pl.program_id(0)
    tt = h_ref.shape[0]
    rows, d = tt * n_batch, n_blocks * blk
    halo = (CONV_WIDTH - 1) * n_batch
    chunk = 2 * d // n_blocks

    @pl.when(i == 0)
    def _():
        x_tb[0:halo, :] = jnp.zeros((halo, d), F32)
        h_carry[...] = jnp.zeros_like(h_carry)
        rx_odd[...] = jnp.zeros_like(rx_odd)
        gy_odd[...] = jnp.zeros_like(gy_odd)

    def body(rx_cur, gy_cur, rx_prev, gy_prev):
        half_l2 = (-0.5 * RG_C * LOG2E) * (jnp.maximum(-lam_ref[...], 0.0)
                                           + jnp.log1p(jnp.exp(-jnp.abs(lam_ref[...]))))
        half_ba = 0.5 * ba_ref[...]
        half_bx = 0.5 * bx_ref[...]
        hb = jnp.concatenate([h_ref[:, b * d:(b + 1) * d].astype(BF16) for b in range(n_batch)],
                             axis=0)

        def conv_and_gate_logits(n):
            sl = slice(n * blk, (n + 1) * blk)
            for t in range(tt):
                x_tb[halo + t * n_batch:halo + (t + 1) * n_batch, sl] = (
                    rx_prev[n, pl.ds(t, n_batch, stride=pitch), :])
            half_xc = 0.5 * cb_ref[:, sl] + (0.5 * cw_ref[0:1, sl]) * x_tb[0:rows, sl]
            for k in range(1, CONV_WIDTH):
                half_xc = half_xc + (0.5 * cw_ref[k:k + 1, sl]) * x_tb[k * n_batch:k * n_batch + rows, sl]
            x_tb[0:halo, sl] = x_tb[rows:rows + halo, sl]
            return half_xc, jnp.dot(half_xc.astype(BF16), wax_ref[n], preferred_element_type=F32)

        staged = conv_and_gate_logits(0)
        for n in range(n_blocks):
            sl = slice(n * blk, (n + 1) * blk)
            half_x, g = staged
            if n + 1 < n_blocks:
                staged = conv_and_gate_logits(n + 1)
            t_r = jnp.tanh(g[:, :blk] + half_ba[:, sl])
            t_i = jnp.tanh(g[:, blk:] + half_bx[:, sl])
            a = jnp.exp2(half_l2[:, sl] + half_l2[:, sl] * t_r)
            one_m_a2 = 1.0 - a * a
            mult = one_m_a2 * lax.rsqrt(jnp.maximum(one_m_a2, SQRT_ARG_FLOOR))
            u = mult * (half_x + half_x * t_i)

            h = h_carry[:, sl]
            for t in range(tt):
                h = a[t * n_batch:(t + 1) * n_batch, :] * h + u[t * n_batch:(t + 1) * n_batch, :]
                h_bt[n, pl.ds(t, n_batch, stride=pitch), :] = h
            h_carry[:, sl] = jnp.where(i > 0, h, 0.0)

            for b in range(n_batch):
                o_ref[:, b * d + n * blk:b * d + (n + 1) * blk] = (
                    h_bt[n, b * pitch:b * pitch + tt, :] * gy_prev[b * tt:(b + 1) * tt, sl]
                ).astype(o_ref.dtype)

            zc = jnp.dot(hb, w_ref[:, n * chunk:(n + 1) * chunk], preferred_element_type=F32)
            for sub in range(chunk // blk):
                col = n * chunk + sub * blk
                piece = zc[:, sub * blk:(sub + 1) * blk]
                if col < d:
                    for b in range(n_batch):
                        rx_cur[col // blk, b * pitch:b * pitch + tt, :] = piece[b * tt:(b + 1) * tt, :]
                else:
                    gy_cur[:, col - d:col - d + blk] = _gelu_tanh(piece)

    _by_parity(i, lambda cur, prev: body(*cur, *prev), (rx_even, gy_even), (rx_odd, gy_odd))


def _rnn_branch(h2, w_r, conv_w, conv_b, w_ax, b_a, b_x, lam, l, *, B, S, D, n_blocks, tt):
    rows = tt * B
    blk = D // n_blocks
    pitch = _row_pitch(tt)
    n_tiles = S // tt
    kern = functools.partial(_rnn_kernel, n_batch=B, n_blocks=n_blocks, blk=blk, pitch=pitch)
    slabs = pltpu.VMEM((n_blocks, B * pitch, blk), F32)
    return pl.pallas_call(
        kern,
        grid=(n_tiles + 1,),
        in_specs=[pl.BlockSpec((tt, B * D), lambda i: (jnp.minimum(i, n_tiles - 1), 0)),
                  _layer_spec((D, 2 * D), l),
                  _layer_spec((CONV_WIDTH, D), l),
                  _layer_spec((1, D), l),
                  _layer_spec((n_blocks, blk, 2 * blk), l),
                  _layer_spec((1, D), l), _layer_spec((1, D), l), _layer_spec((1, D), l)],
        out_specs=pl.BlockSpec((tt, B * D), lambda i: (jnp.maximum(i - 1, 0), 0)),
        out_shape=jax.ShapeDtypeStruct((S, B * D), BF16),
        scratch_shapes=[slabs, slabs,
                        pltpu.VMEM((rows, D), F32), pltpu.VMEM((rows, D), F32),
                        pltpu.VMEM((rows + (CONV_WIDTH - 1) * B, D), F32),
                        slabs,
                        pltpu.VMEM((B, D), F32)],
        compiler_params=_params("arbitrary"),
        name="rglru_branch",
    )(h2, w_r, conv_w, conv_b, w_ax, b_a, b_x, lam)


def _by_parity(step, body, even, odd):
    parity = lax.rem(step, 2)

    @pl.when(parity == 0)
    def _():
        body(even, odd)

    @pl.when(parity == 1)
    def _():
        body(odd, even)


def _tile_maps(n_tiles, tiles_per_batch):
    def head(g):
        t = jnp.minimum(g, n_tiles - 1)
        return t % tiles_per_batch, t // tiles_per_batch

    def tail(g):
        t = jnp.maximum(g - 1, 0)
        return t % tiles_per_batch, t // tiles_per_batch
    return head, tail


def _merge_kernel(h_ref, att_ref, rnn_ref, wg_ref, wa_ref, wr_ref, wo_ref, bm_ref, g_ref, b_ref,
                  o_ref, y_even, y_odd, *, n_heads, alpha):
    step = pl.program_id(0)

    @pl.when(step == 0)
    def _():
        y_odd[...] = jnp.zeros_like(y_odd)

    def body(y_cur, y_prev):
        h = h_ref[...]
        d = h.shape[1]
        gates = jnp.dot(h.astype(BF16), wg_ref[...], preferred_element_type=F32)
        o_ref[...] = _layer_norm(y_prev[...], g_ref[...], b_ref[...])
        att = jnp.concatenate([att_ref[j] for j in range(n_heads)], axis=1)
        ya = jnp.dot(att, wa_ref[...], preferred_element_type=F32)
        yb = jnp.dot(rnn_ref[...], wr_ref[...], preferred_element_type=F32)
        merged = (_sigmoid(gates[:, :d] + bm_ref[0:1, :]) * ya
                  + _sigmoid(gates[:, d:] + bm_ref[1:2, :]) * yb)
        m = jnp.dot(merged.astype(BF16), wo_ref[...], preferred_element_type=F32)
        y_cur[...] = alpha * h + m

    _by_parity(step, body, y_even, y_odd)


def _merge(h2, att, rnn2, w_g, w_a, w_r, w_o, b_m, ln_g, ln_b, l, *, B, S, D, n_heads, head_dim,
           ts, alpha):
    kern = functools.partial(_merge_kernel, n_heads=n_heads, alpha=alpha)
    d_att = n_heads * head_dim
    n_tiles = B * (S // ts)
    head, tail = _tile_maps(n_tiles, S // ts)
    tok = pl.BlockSpec((ts, D), lambda g: head(g))
    return pl.pallas_call(
        kern,
        grid=(n_tiles + 1,),
        in_specs=[tok,
                  pl.BlockSpec((n_heads, None, ts, head_dim),
                               lambda g: (0, head(g)[1], head(g)[0], 0)),
                  tok,
                  _layer_spec((D, 2 * D), l), _layer_spec((d_att, D), l),
                  _layer_spec((D, D), l), _layer_spec((D, D), l),
                  _layer_spec((2, D), l), _layer_spec((1, D), l), _layer_spec((1, D), l)],
        out_specs=pl.BlockSpec((ts, D), lambda g: tail(g)),
        out_shape=jax.ShapeDtypeStruct((S, B * D), F32),
        scratch_shapes=[pltpu.VMEM((ts, D), F32), pltpu.VMEM((ts, D), F32)],
        compiler_params=_params("arbitrary"),
        name="merge_out",
    )(h2, att, rnn2, w_g, w_a, w_r, w_o, b_m, ln_g, ln_b)


def _ffn_kernel(h_ref, p_ref, wi_ref, wo_ref, g1_ref, b1_ref, wpg_ref, bpg_ref, wp_ref,
                g2_ref, b2_ref, o_ref, y_even, y_odd, *, d_ff, chunks, alpha):
    step = pl.program_id(0)

    @pl.when(step == 0)
    def _():
        y_odd[...] = jnp.zeros_like(y_odd)

    def body(y_cur, y_prev):
        h = h_ref[...]
        hb = h.astype(BF16)

        def up(lo, hi):
            return (jnp.dot(hb, wi_ref[:, lo:hi], preferred_element_type=F32),
                    jnp.dot(hb, wi_ref[:, d_ff + lo:d_ff + hi], preferred_element_type=F32))

        def down(half_g, hu, lo, hi):
            act = ((half_g + half_g * jnp.tanh(half_g)) * hu).astype(BF16)
            return jnp.dot(act, wo_ref[lo:hi, :], preferred_element_type=F32)

        def finish_previous_tile(h1, gate_logits, emb):
            o_ref[...] = _layer_norm(alpha * h1 + _sigmoid(gate_logits + bpg_ref[...]) * emb,
                                     g2_ref[...], b2_ref[...])

        f = down(*up(*chunks[0]), *chunks[0])
        h1 = _layer_norm(y_prev[...], g1_ref[...], b1_ref[...])
        gate_logits = jnp.dot(h1.astype(BF16), wpg_ref[...], preferred_element_type=F32)
        emb = jnp.dot(p_ref[...].astype(BF16), wp_ref[...], preferred_element_type=F32)
        for lo, hi in chunks[1:2]:
            f = f + down(*up(lo, hi), lo, hi)
        finish_previous_tile(h1, gate_logits, emb)
        for lo, hi in chunks[2:]:
            f = f + down(*up(lo, hi), lo, hi)
        y_cur[...] = alpha * h + f

    _by_parity(step, body, y_even, y_odd)


def _ffn_chunks(d_ff, width):
    edges = list(range(0, d_ff, width)) + [d_ff]
    return tuple(zip(edges[:-1], edges[1:]))


def _ffn_ple(h2, p, w_i, w_o, g1, b1, w_pg, b_pg, w_p, g2, b2, l, *, B, S, D, d_ff, d_ple, ts,
             alpha, batch_major_out):
    kern = functools.partial(_ffn_kernel, d_ff=d_ff, chunks=_ffn_chunks(d_ff, FFN_CHUNK_COLS), alpha=alpha)
    n_tiles = B * (S // ts)
    head, tail = _tile_maps(n_tiles, S // ts)
    if batch_major_out:
        out_spec = pl.BlockSpec((None, ts, D), lambda g: (tail(g)[1], tail(g)[0], 0))
        out_shape = jax.ShapeDtypeStruct((B, S, D), F32)
    else:
        out_spec = pl.BlockSpec((ts, D), lambda g: tail(g))
        out_shape = jax.ShapeDtypeStruct((S, B * D), F32)
    vec = _layer_spec((1, D), l)
    return pl.pallas_call(
        kern,
        grid=(n_tiles + 1,),
        in_specs=[pl.BlockSpec((ts, D), lambda g: head(g)),
                  pl.BlockSpec((None, None, ts, d_ple), lambda g: (l, tail(g)[1], tail(g)[0], 0)),
                  _layer_spec((D, 2 * d_ff), l), _layer_spec((d_ff, D), l), vec, vec,
                  _layer_spec((D, D), l), vec, _layer_spec((d_ple, D), l), vec, vec],
        out_specs=out_spec,
        out_shape=out_shape,
        scratch_shapes=[pltpu.VMEM((ts, D), F32), pltpu.VMEM((ts, D), F32)],
        compiler_params=_params("arbitrary"),
        name="ffn_ple",
    )(h2, p, w_i, w_o, g1, b1, w_pg, b_pg, w_p, g2, b2)


def kernel(x, p, ln_in_g, ln_in_b, w_in, b_forget, conv_w, conv_b, rg_w_a, rg_b_a, rg_w_x, rg_b_x,
           rg_lambda, w_branch_att, w_branch_rnn, b_merge, w_out, ln_mix_g, ln_mix_b, w_ffn_in,
           w_ffn_out, ln_ffn_g, ln_ffn_b, w_ple, w_ple_gate, b_ple_gate, ln_ple_g, ln_ple_b):
    B, S, D = x.shape
    L = w_in.shape[0]
    H = b_forget.shape[1]
    d_att = w_branch_att.shape[1]
    head_dim = d_att // H
    n_blocks, blk = rg_w_a.shape[1], rg_w_a.shape[2]
    d_ff = w_ffn_out.shape[1]
    d_ple = w_ple.shape[1]
    assert head_dim == LANES and blk == LANES and D == n_blocks * blk and d_att == D
    assert H <= KEY_TERM_STRIDE and B % SUBLANES == 0
    alpha = float((2 * L) ** 0.25)

    ts = min(ROW_TILE, S)
    tq = min(ROW_TILE, S)
    tt = max(1, min(S, ROW_TILE // B))

    o_f, o_rx = 3 * d_att, 3 * d_att + H
    o_g = o_rx + 2 * D
    w_qkv = jnp.pad(w_in[:, :, :o_rx], ((0, 0), (0, 0), (0, LANES - H))).astype(BF16)
    b_fc = jnp.pad(b_forget, ((0, 0), (0, LANES - H))).reshape(L, 1, LANES)
    w_r = w_in[:, :, o_rx:o_g].astype(BF16)
    w_g = w_in[:, :, o_g:].astype(BF16)
    w_ax = jnp.concatenate([rg_w_a, rg_w_x], axis=-1).astype(BF16)
    w_ba, w_br, w_o = (w.astype(BF16) for w in (w_branch_att, w_branch_rnn, w_out))
    gate_half = jnp.where(jnp.arange(2 * d_ff) < d_ff, 0.5, 1.0).astype(F32)
    w_fi, w_fo = (w_ffn_in * gate_half).astype(BF16), w_ffn_out.astype(BF16)
    w_pg, w_p = w_ple_gate.astype(BF16), w_ple.astype(BF16)
    vec = lambda a: a.reshape(L, 1, D)

    h2 = _ln_in(x, ln_in_g, ln_in_b, min(2 * ROW_TILE, S))
    for l in range(L):
        qkv, ccol, kf = _qkv_proj(h2, w_qkv, b_fc, l, B=B, S=S, D=D,
                                  n_heads=H, head_dim=head_dim, tm=ts)
        att = _attention(qkv, ccol, kf, B=B, S=S, n_heads=H, head_dim=head_dim, tq=tq, group=H)
        rnn = _rnn_branch(h2, w_r, conv_w, vec(conv_b), w_ax, vec(rg_b_a),
                          vec(rg_b_x), vec(rg_lambda), l, B=B, S=S, D=D, n_blocks=n_blocks, tt=tt)
        h2 = _merge(h2, att, rnn, w_g, w_ba, w_br, w_o, b_merge,
                    vec(ln_mix_g), vec(ln_mix_b), l, B=B, S=S, D=D, n_heads=H, head_dim=head_dim,
                    ts=ts, alpha=alpha)
        h2 = _ffn_ple(h2, p, w_fi, w_fo, vec(ln_ffn_g), vec(ln_ffn_b), w_pg, vec(b_ple_gate), w_p,
                      vec(ln_ple_g), vec(ln_ple_b), l, B=B, S=S, D=D, d_ff=d_ff, d_ple=d_ple,
                      ts=ts, alpha=alpha, batch_major_out=(l == L - 1))
    return h2
```

```python
import functools
import math

import jax
import jax.numpy as jnp
from jax import lax
from jax.experimental import pallas as pl
from jax.experimental.pallas import tpu as pltpu

LN_EPS = 1e-5
RG_C = 8.0
CONV_WIDTH = 4
LOG2E = 1.4426950408889634
SQRT_ARG_FLOOR = 1e-30
LANES = 128
SUBLANES = 8
KEY_TERM_SHIFT = 3
KEY_TERM_STRIDE = 1 << KEY_TERM_SHIFT
VMEM_LIMIT_BYTES = 56 * 1024 * 1024
ROW_TILE = 512
FFN_CHUNK_COLS = 1024

F32 = jnp.float32
BF16 = jnp.bfloat16


def _params(*semantics):
    return pltpu.CompilerParams(dimension_semantics=semantics,
                                vmem_limit_bytes=VMEM_LIMIT_BYTES)


def _layer_spec(tail, l):
    zeros = (0,) * len(tail)
    return pl.BlockSpec((None,) + tuple(tail), lambda *_: (l,) + zeros,
                        pipeline_mode=pl.Buffered(1))


def _const_spec(shape):
    zeros = (0,) * len(shape)
    return pl.BlockSpec(tuple(shape), lambda *_: zeros, pipeline_mode=pl.Buffered(1))


def _sigmoid(x):
    return 1.0 / (1.0 + jnp.exp(-x))


def _layer_norm(x, g, b):
    mu = jnp.mean(x, axis=-1, keepdims=True)
    xc = x - mu
    var = jnp.mean(xc * xc, axis=-1, keepdims=True)
    return xc * lax.rsqrt(var + LN_EPS) * g + b


def _split3(x):
    hi = x.astype(BF16)
    r1 = x - hi.astype(F32)
    mid = r1.astype(BF16)
    lo = (r1 - mid.astype(F32)).astype(BF16)
    return hi, mid, lo


def _ln_in_kernel(x_ref, g_ref, b_ref, o_ref):
    o_ref[...] = _layer_norm(x_ref[...], g_ref[...], b_ref[...])


def _ln_in(x, g, b, ts):
    B, S, D = x.shape
    return pl.pallas_call(
        _ln_in_kernel,
        grid=(B, S // ts),
        in_specs=[pl.BlockSpec((None, ts, D), lambda b_, i: (b_, i, 0)),
                  _const_spec((1, D)), _const_spec((1, D))],
        out_specs=pl.BlockSpec((ts, D), lambda b_, i: (i, b_)),
        out_shape=jax.ShapeDtypeStruct((S, B * D), F32),
        compiler_params=_params("parallel", "parallel"),
        name="ln_in",
    )(x, g.reshape(1, D), b.reshape(1, D))


def _qkv_kernel(h_ref, w_ref, bfc_ref, qkv_ref, ccol_ref, kf_ref, carry_c,
                *, n_heads, head_dim, q_scale):
    i = pl.program_id(1)

    @pl.when(i == 0)
    def _():
        carry_c[...] = jnp.zeros_like(carry_c)

    hb = h_ref[...].astype(BF16)
    tm = hb.shape[0]
    d_qkv = 3 * n_heads * head_dim
    fc = jnp.dot(hb, w_ref[:, d_qkv:], preferred_element_type=F32) + bfc_ref[...]
    z = jnp.dot(hb, w_ref[:, :d_qkv], preferred_element_type=F32)
    for j in range(3 * n_heads):
        blk = z[:, j * head_dim:(j + 1) * head_dim]
        if j < n_heads:
            blk = blk * q_scale
        qkv_ref[j] = blk.astype(BF16)

    def log2_forget(f):
        return (jnp.minimum(f, 0.0) - jnp.log1p(jnp.exp(-jnp.abs(f)))) * LOG2E

    row = lax.broadcasted_iota(jnp.int32, (tm, tm), 0)
    col = lax.broadcasted_iota(jnp.int32, (tm, tm), 1)

    lower = jnp.where(col <= row, 1.0, 0.0).astype(BF16)
    cc = jnp.dot(lower, jnp.concatenate(_split3(log2_forget(fc)), axis=1),
                 preferred_element_type=F32)
    cc = cc[:, :LANES] + cc[:, LANES:2 * LANES] + cc[:, 2 * LANES:] + carry_c[...]
    ccol_ref[...] = cc
    carry_c[...] = cc[tm - 1:tm, :]

    lane = lax.broadcasted_iota(jnp.int32, (tm, LANES), 1)
    pieces = [jnp.where(lane < n_heads, piece.astype(F32), 0.0) for piece in _split3(-cc)]
    kf = pieces[0]
    for term in (1, 2):
        kf = kf + pltpu.roll(pieces[term], term * KEY_TERM_STRIDE, 1)
    kf_ref[...] = kf.astype(BF16)


def _qkv_proj(h2, w_qkv, b_fc, l, *, B, S, D, n_heads, head_dim, tm):
    d_att = n_heads * head_dim
    kern = functools.partial(_qkv_kernel, n_heads=n_heads, head_dim=head_dim,
                             q_scale=LOG2E / math.sqrt(head_dim))
    return pl.pallas_call(
        kern,
        grid=(B, S // tm),
        in_specs=[pl.BlockSpec((tm, D), lambda b_, i: (i, b_)),
                  _layer_spec((D, 3 * d_att + LANES), l),
                  _layer_spec((1, LANES), l)],
        out_specs=[pl.BlockSpec((3 * n_heads, None, tm, head_dim), lambda b_, i: (0, b_, i, 0)),
                   pl.BlockSpec((None, tm, LANES), lambda b_, i: (b_, i, 0)),
                   pl.BlockSpec((None, tm, LANES), lambda b_, i: (b_, i, 0))],
        out_shape=[jax.ShapeDtypeStruct((3 * n_heads, B, S, head_dim), BF16),
                   jax.ShapeDtypeStruct((B, S, LANES), F32),
                   jax.ShapeDtypeStruct((B, S, LANES), BF16)],
        scratch_shapes=[pltpu.VMEM((1, LANES), F32)],
        compiler_params=_params("parallel", "arbitrary"),
        name="qkv_proj",
    )(h2, w_qkv, b_fc)


def _attn_kernel(q_ref, k_ref, v_ref, cq_ref, kf_ref, o_ref, *, n_heads, tq, group, wide):
    i = pl.program_id(1)
    n_q_tiles = kf_ref.shape[0] // tq
    lane = lax.broadcasted_iota(jnp.int32, (tq, LANES), 1)
    r_idx = lax.broadcasted_iota(jnp.int32, (tq, tq), 0)
    c_idx = lax.broadcasted_iota(jnp.int32, (tq, tq), 1)
    causal_bias = jnp.where(c_idx <= r_idx, 0.0, -jnp.inf)
    cq_all = cq_ref[...]
    nt = (((1,), (1,)), ((), ()))

    def chain_inputs(h):
        pick = jnp.where(lane < 3 * KEY_TERM_STRIDE,
                         jnp.where((lane & (KEY_TERM_STRIDE - 1)) == h, 1.0, 0.0), 0.0)
        q_aug = jnp.concatenate([q_ref[h], pick.astype(BF16)], axis=1)
        c_q = jnp.sum(jnp.where(lane == h, cq_all, 0.0), axis=1, keepdims=True)
        return q_aug, c_q

    def operands(h, start, width):
        keys = pl.ds(start, width)
        k_aug = jnp.concatenate([k_ref[h, keys, :], kf_ref[keys, :]], axis=1)
        v_aug = jnp.concatenate([v_ref[h, keys, :], jnp.ones((width, LANES), BF16)], axis=1)
        return k_aug, v_aug

    def advance(state, s, v_aug, c_q):
        row_max = c_q + jnp.max(s, axis=1, keepdims=True)
        if state is None:
            p = jnp.exp2(s + (c_q - row_max))
            return row_max, jnp.dot(p.astype(BF16), v_aug, preferred_element_type=F32)
        m, acc = state
        m_new = jnp.maximum(m, row_max)
        alpha = jnp.exp2(m - m_new)
        p = jnp.exp2(s + (c_q - m_new))
        return m_new, alpha * acc + jnp.dot(p.astype(BF16), v_aug, preferred_element_type=F32)

    def head_group(g, _, *, n_full):
        heads = [g * group + u for u in range(group)]
        inputs = [chain_inputs(h) for h in heads]
        state = [None] * group
        spans, j = [], 0
        while j < n_full:
            step = min(wide, n_full - j)
            spans.append((j * tq, step * tq, False))
            j += step
        spans.append((n_full * tq, tq, True))
        for start, width, diagonal in spans:
            for u, (h, (q_aug, c_q)) in enumerate(zip(heads, inputs)):
                k_aug, v_aug = operands(h, start, width)
                s = lax.dot_general(q_aug, k_aug, nt, preferred_element_type=F32)
                if diagonal:
                    s = s + causal_bias
                state[u] = advance(state[u], s, v_aug, c_q)
        for u, h in enumerate(heads):
            acc = state[u][1]
            o_ref[h] = (acc[:, :LANES] / acc[:, LANES:]).astype(o_ref.dtype)
        return 0

    for n_full in range(n_q_tiles):
        @pl.when(i == n_full)
        def _(n_full=n_full):
            lax.fori_loop(0, n_heads // group, functools.partial(head_group, n_full=n_full), 0)


def _attention(qkv, ccol, kf, *, B, S, n_heads, head_dim, tq, group):
    H = n_heads
    kern = functools.partial(_attn_kernel, n_heads=H, tq=tq, group=group, wide=3)
    return pl.pallas_call(
        kern,
        grid=(B, S // tq),
        in_specs=[pl.BlockSpec((H, None, tq, head_dim), lambda b_, i: (0, b_, i, 0)),
                  pl.BlockSpec((H, None, S, head_dim), lambda b_, i: (1, b_, 0, 0)),
                  pl.BlockSpec((H, None, S, head_dim), lambda b_, i: (2, b_, 0, 0)),
                  pl.BlockSpec((None, tq, LANES), lambda b_, i: (b_, i, 0)),
                  pl.BlockSpec((None, S, LANES), lambda b_, i: (b_, 0, 0))],
        out_specs=pl.BlockSpec((H, None, tq, head_dim), lambda b_, i: (0, b_, i, 0)),
        out_shape=jax.ShapeDtypeStruct((H, B, S, head_dim), BF16),
        compiler_params=_params("parallel", "arbitrary"),
        name="fox_attention",
    )(qkv, qkv, qkv, ccol, kf)


def _row_pitch(tt):
    groups = tt // SUBLANES + 1
    return SUBLANES * (groups if groups % 2 else groups + 1)


def _gelu_tanh(x):
    c = math.sqrt(2.0 / math.pi)
    half_x = 0.5 * x
    return half_x + half_x * jnp.tanh(x * (c + (c * 0.044715) * (x * x)))


def _rnn_kernel(h_ref, w_ref, cw_ref, cb_ref, wax_ref, ba_ref, bx_ref, lam_ref,
                o_ref, rx_even, rx_odd, gy_even, gy_odd, x_tb, h_bt, h_carry,
                *, n_batch, n_blocks, blk, pitch, n_tiles):
    i = pl.program_id(0)
    tt = h_ref.shape[0]
    rows, d = tt * n_batch, n_blocks * blk
    halo = (CONV_WIDTH - 1) * n_batch
    chunk = 2 * d // n_blocks

    @pl.when(i == 0)
    def _():
        x_tb[0:halo, :] = jnp.zeros((halo, d), F32)
        h_carry[...] = jnp.zeros_like(h_carry)
        rx_odd[...] = jnp.zeros_like(rx_odd)
        gy_odd[...] = jnp.zeros_like(gy_odd)

    def body(cur, prev, project):
        (rx_cur, gy_cur), (rx_prev, gy_prev) = cur, prev
        half_l2 = (-0.5 * RG_C * LOG2E) * (jnp.maximum(-lam_ref[...], 0.0)
                                           + jnp.log1p(jnp.exp(-jnp.abs(lam_ref[...]))))
        half_ba = 0.5 * ba_ref[...]
        half_bx = 0.5 * bx_ref[...]
        if project:
            hb = jnp.concatenate([h_ref[:, b * d:(b + 1) * d].astype(BF16) for b in range(n_batch)],
                                 axis=0)

        def conv_and_gate_logits(n):
            sl = slice(n * blk, (n + 1) * blk)
            for t in range(tt):
                x_tb[halo + t * n_batch:halo + (t + 1) * n_batch, sl] = (
                    rx_prev[n, pl.ds(t, n_batch, stride=pitch), :])
            half_xc = 0.5 * cb_ref[:, sl] + (0.5 * cw_ref[0:1, sl]) * x_tb[0:rows, sl]
            for k in range(1, CONV_WIDTH):
                half_xc = half_xc + (0.5 * cw_ref[k:k + 1, sl]) * x_tb[k * n_batch:k * n_batch + rows, sl]
            x_tb[0:halo, sl] = x_tb[rows:rows + halo, sl]
            return half_xc, jnp.dot(half_xc.astype(BF16), wax_ref[n], preferred_element_type=F32)

        staged = conv_and_gate_logits(0)
        for n in range(n_blocks):
            sl = slice(n * blk, (n + 1) * blk)
            half_x, g = staged
            if n + 1 < n_blocks:
                staged = conv_and_gate_logits(n + 1)
            t_r = jnp.tanh(g[:, :blk] + half_ba[:, sl])
            t_i = jnp.tanh(g[:, blk:] + half_bx[:, sl])
            a = jnp.exp2(half_l2[:, sl] + half_l2[:, sl] * t_r)
            one_m_a2 = 1.0 - a * a
            mult = one_m_a2 * lax.rsqrt(jnp.maximum(one_m_a2, SQRT_ARG_FLOOR))
            u = mult * (half_x + half_x * t_i)

            h = h_carry[:, sl]
            for t in range(tt):
                h = a[t * n_batch:(t + 1) * n_batch, :] * h + u[t * n_batch:(t + 1) * n_batch, :]
                h_bt[n, pl.ds(t, n_batch, stride=pitch), :] = h
            h_carry[:, sl] = jnp.where(i > 0, h, 0.0)

            for b in range(n_batch):
                o_ref[:, b * d + n * blk:b * d + (n + 1) * blk] = (
                    h_bt[n, b * pitch:b * pitch + tt, :] * gy_prev[b * tt:(b + 1) * tt, sl]
                ).astype(o_ref.dtype)

            if not project:
                continue
            zc = jnp.dot(hb, w_ref[:, n * chunk:(n + 1) * chunk], preferred_element_type=F32)
            for sub in range(chunk // blk):
                col = n * chunk + sub * blk
                piece = zc[:, sub * blk:(sub + 1) * blk]
                if col < d:
                    for b in range(n_batch):
                        rx_cur[col // blk, b * pitch:b * pitch + tt, :] = piece[b * tt:(b + 1) * tt, :]
                else:
                    gy_cur[:, col - d:col - d + blk] = _gelu_tanh(piece)

    _by_parity(i, n_tiles, body, (rx_even, gy_even), (rx_odd, gy_odd))


def _rnn_branch(h2, w_r, conv_w, conv_b, w_ax, b_a, b_x, lam, l, *, B, S, D, n_blocks, tt):
    rows = tt * B
    blk = D // n_blocks
    pitch = _row_pitch(tt)
    n_tiles = S // tt
    kern = functools.partial(_rnn_kernel, n_batch=B, n_blocks=n_blocks, blk=blk, pitch=pitch,
                             n_tiles=n_tiles)
    slabs = pltpu.VMEM((n_blocks, B * pitch, blk), F32)
    return pl.pallas_call(
        kern,
        grid=(n_tiles + 1,),
        in_specs=[pl.BlockSpec((tt, B * D), lambda i: (jnp.minimum(i, n_tiles - 1), 0)),
                  _layer_spec((D, 2 * D), l),
                  _layer_spec((CONV_WIDTH, D), l),
                  _layer_spec((1, D), l),
                  _layer_spec((n_blocks, blk, 2 * blk), l),
                  _layer_spec((1, D), l), _layer_spec((1, D), l), _layer_spec((1, D), l)],
        out_specs=pl.BlockSpec((tt, B * D), lambda i: (jnp.maximum(i - 1, 0), 0)),
        out_shape=jax.ShapeDtypeStruct((S, B * D), BF16),
        scratch_shapes=[slabs, slabs,
                        pltpu.VMEM((rows, D), F32), pltpu.VMEM((rows, D), F32),
                        pltpu.VMEM((rows + (CONV_WIDTH - 1) * B, D), F32),
                        slabs,
                        pltpu.VMEM((B, D), F32)],
        compiler_params=_params("arbitrary"),
        name="rglru_branch",
    )(h2, w_r, conv_w, conv_b, w_ax, b_a, b_x, lam)


def _by_parity(step, n_tiles, body, even, odd):
    parity = lax.rem(step, 2)
    roles = ((even, odd), (odd, even))
    for par, (cur, prev) in enumerate(roles):
        if par == n_tiles % 2:
            pl.when(jnp.logical_and(parity == par, step < n_tiles))(
                functools.partial(body, cur, prev, True))
            pl.when(step == n_tiles)(functools.partial(body, cur, prev, False))
        else:
            pl.when(parity == par)(functools.partial(body, cur, prev, True))


def _tile_maps(n_tiles, tiles_per_batch):
    def head(g):
        t = jnp.minimum(g, n_tiles - 1)
        return t % tiles_per_batch, t // tiles_per_batch

    def tail(g):
        t = jnp.maximum(g - 1, 0)
        return t % tiles_per_batch, t // tiles_per_batch
    return head, tail


def _merge_kernel(h_ref, att_ref, rnn_ref, wg_ref, wa_ref, wr_ref, wo_ref, bm_ref, g_ref, b_ref,
                  o_ref, y_even, y_odd, *, n_heads, alpha, n_tiles):
    step = pl.program_id(0)

    @pl.when(step == 0)
    def _():
        y_odd[...] = jnp.zeros_like(y_odd)

    def body(y_cur, y_prev, project):
        if project:
            h = h_ref[...]
            d = h.shape[1]
            gates = jnp.dot(h.astype(BF16), wg_ref[...], preferred_element_type=F32)
        o_ref[...] = _layer_norm(y_prev[...], g_ref[...], b_ref[...])
        if project:
            att = jnp.concatenate([att_ref[j] for j in range(n_heads)], axis=1)
            ya = jnp.dot(att, wa_ref[...], preferred_element_type=F32)
            yb = jnp.dot(rnn_ref[...], wr_ref[...], preferred_element_type=F32)
            merged = (_sigmoid(gates[:, :d] + bm_ref[0:1, :]) * ya
                      + _sigmoid(gates[:, d:] + bm_ref[1:2, :]) * yb)
            m = jnp.dot(merged.astype(BF16), wo_ref[...], preferred_element_type=F32)
            y_cur[...] = alpha * h + m

    _by_parity(step, n_tiles, body, y_even, y_odd)


def _merge(h2, att, rnn2, w_g, w_a, w_r, w_o, b_m, ln_g, ln_b, l, *, B, S, D, n_heads, head_dim,
           ts, alpha):
    kern = functools.partial(_merge_kernel, n_heads=n_heads, alpha=alpha, n_tiles=B * (S // ts))
    d_att = n_heads * head_dim
    n_tiles = B * (S // ts)
    head, tail = _tile_maps(n_tiles, S // ts)
    tok = pl.BlockSpec((ts, D), lambda g: head(g))
    return pl.pallas_call(
        kern,
        grid=(n_tiles + 1,),
        in_specs=[tok,
                  pl.BlockSpec((n_heads, None, ts, head_dim),
                               lambda g: (0, head(g)[1], head(g)[0], 0)),
                  tok,
                  _layer_spec((D, 2 * D), l), _layer_spec((d_att, D), l),
                  _layer_spec((D, D), l), _layer_spec((D, D), l),
                  _layer_spec((2, D), l), _layer_spec((1, D), l), _layer_spec((1, D), l)],
        out_specs=pl.BlockSpec((ts, D), lambda g: tail(g)),
        out_shape=jax.ShapeDtypeStruct((S, B * D), F32),
        scratch_shapes=[pltpu.VMEM((ts, D), F32), pltpu.VMEM((ts, D), F32)],
        compiler_params=_params("arbitrary"),
        name="merge_out",
    )(h2, att, rnn2, w_g, w_a, w_r, w_o, b_m, ln_g, ln_b)


def _ffn_kernel(h_ref, p_ref, wi_ref, wo_ref, g1_ref, b1_ref, wpg_ref, bpg_ref, wp_ref,
                g2_ref, b2_ref, o_ref, y_even, y_odd, *, d_ff, chunks, alpha, n_tiles):
    step = pl.program_id(0)

    @pl.when(step == 0)
    def _():
        y_odd[...] = jnp.zeros_like(y_odd)

    def body(y_cur, y_prev, project):
        def previous_tile_inputs():
            h1 = _layer_norm(y_prev[...], g1_ref[...], b1_ref[...])
            return (h1, jnp.dot(h1.astype(BF16), wpg_ref[...], preferred_element_type=F32),
                    jnp.dot(p_ref[...].astype(BF16), wp_ref[...], preferred_element_type=F32))

        if not project:
            finish_previous_tile(*previous_tile_inputs())
            return
        h = h_ref[...]
        hb = h.astype(BF16)

        def up(lo, hi):
            return (jnp.dot(hb, wi_ref[:, lo:hi], preferred_element_type=F32),
                    jnp.dot(hb, wi_ref[:, d_ff + lo:d_ff + hi], preferred_element_type=F32))

        def down(half_g, hu, lo, hi):
            act = ((half_g + half_g * jnp.tanh(half_g)) * hu).astype(BF16)
            return jnp.dot(act, wo_ref[lo:hi, :], preferred_element_type=F32)

        f = down(*up(*chunks[0]), *chunks[0])
        tail_inputs = previous_tile_inputs()
        for lo, hi in chunks[1:2]:
            f = f + down(*up(lo, hi), lo, hi)
        finish_previous_tile(*tail_inputs)
        for lo, hi in chunks[2:]:
            f = f + down(*up(lo, hi), lo, hi)
        y_cur[...] = alpha * h + f

    def finish_previous_tile(h1, gate_logits, emb):
        o_ref[...] = _layer_norm(alpha * h1 + _sigmoid(gate_logits + bpg_ref[...]) * emb,
                                 g2_ref[...], b2_ref[...])

    _by_parity(step, n_tiles, body, y_even, y_odd)


def _ffn_chunks(d_ff, width):
    edges = list(range(0, d_ff, width)) + [d_ff]
    return tuple(zip(edges[:-1], edges[1:]))


def _ffn_ple(h2, p, w_i, w_o, g1, b1, w_pg, b_pg, w_p, g2, b2, l, *, B, S, D, d_ff, d_ple, ts,
             alpha, batch_major_out):
    n_tiles = B * (S // ts)
    kern = functools.partial(_ffn_kernel, d_ff=d_ff, chunks=_ffn_chunks(d_ff, FFN_CHUNK_COLS),
                             alpha=alpha, n_tiles=n_tiles)
    head, tail = _tile_maps(n_tiles, S // ts)
    if batch_major_out:
        out_spec = pl.BlockSpec((None, ts, D), lambda g: (tail(g)[1], tail(g)[0], 0))
        out_shape = jax.ShapeDtypeStruct((B, S, D), F32)
    else:
        out_spec = pl.BlockSpec((ts, D), lambda g: tail(g))
        out_shape = jax.ShapeDtypeStruct((S, B * D), F32)
    vec = _layer_spec((1, D), l)
    return pl.pallas_call(
        kern,
        grid=(n_tiles + 1,),
        in_specs=[pl.BlockSpec((ts, D), lambda g: head(g)),
                  pl.BlockSpec((None, None, ts, d_ple), lambda g: (l, tail(g)[1], tail(g)[0], 0)),
                  _layer_spec((D, 2 * d_ff), l), _layer_spec((d_ff, D), l), vec, vec,
                  _layer_spec((D, D), l), vec, _layer_spec((d_ple, D), l), vec, vec],
        out_specs=out_spec,
        out_shape=out_shape,
        scratch_shapes=[pltpu.VMEM((ts, D), F32), pltpu.VMEM((ts, D), F32)],
        compiler_params=_params("arbitrary"),
        name="ffn_ple",
    )(h2, p, w_i, w_o, g1, b1, w_pg, b_pg, w_p, g2, b2)


def kernel(x, p, ln_in_g, ln_in_b, w_in, b_forget, conv_w, conv_b, rg_w_a, rg_b_a, rg_w_x, rg_b_x,
           rg_lambda, w_branch_att, w_branch_rnn, b_merge, w_out, ln_mix_g, ln_mix_b, w_ffn_in,
           w_ffn_out, ln_ffn_g, ln_ffn_b, w_ple, w_ple_gate, b_ple_gate, ln_ple_g, ln_ple_b):
    B, S, D = x.shape
    L = w_in.shape[0]
    H = b_forget.shape[1]
    d_att = w_branch_att.shape[1]
    head_dim = d_att // H
    n_blocks, blk = rg_w_a.shape[1], rg_w_a.shape[2]
    d_ff = w_ffn_out.shape[1]
    d_ple = w_ple.shape[1]
    assert head_dim == LANES and blk == LANES and D == n_blocks * blk and d_att == D
    assert H <= KEY_TERM_STRIDE and B % SUBLANES == 0
    alpha = float((2 * L) ** 0.25)

    ts = min(ROW_TILE, S)
    tq = min(ROW_TILE, S)
    tt = max(1, min(S, ROW_TILE // B))

    o_f, o_rx = 3 * d_att, 3 * d_att + H
    o_g = o_rx + 2 * D
    w_qkv = jnp.pad(w_in[:, :, :o_rx], ((0, 0), (0, 0), (0, LANES - H))).astype(BF16)
    b_fc = jnp.pad(b_forget, ((0, 0), (0, LANES - H))).reshape(L, 1, LANES)
    w_r = w_in[:, :, o_rx:o_g].astype(BF16)
    w_g = w_in[:, :, o_g:].astype(BF16)
    w_ax = jnp.concatenate([rg_w_a, rg_w_x], axis=-1).astype(BF16)
    w_ba, w_br, w_o = (w.astype(BF16) for w in (w_branch_att, w_branch_rnn, w_out))
    gate_half = jnp.where(jnp.arange(2 * d_ff) < d_ff, 0.5, 1.0).astype(F32)
    w_fi, w_fo = (w_ffn_in * gate_half).astype(BF16), w_ffn_out.astype(BF16)
    w_pg, w_p = w_ple_gate.astype(BF16), w_ple.astype(BF16)
    vec = lambda a: a.reshape(L, 1, D)

    h2 = _ln_in(x, ln_in_g, ln_in_b, min(2 * ROW_TILE, S))
    for l in range(L):
        qkv, ccol, kf = _qkv_proj(h2, w_qkv, b_fc, l, B=B, S=S, D=D,
                                  n_heads=H, head_dim=head_dim, tm=ts)
        att = _attention(qkv, ccol, kf, B=B, S=S, n_heads=H, head_dim=head_dim, tq=tq, group=H)
        rnn = _rnn_branch(h2, w_r, conv_w, vec(conv_b), w_ax, vec(rg_b_a),
                          vec(rg_b_x), vec(rg_lambda), l, B=B, S=S, D=D, n_blocks=n_blocks, tt=tt)
        h2 = _merge(h2, att, rnn, w_g, w_ba, w_br, w_o, b_merge,
                    vec(ln_mix_g), vec(ln_mix_b), l, B=B, S=S, D=D, n_heads=H, head_dim=head_dim,
                    ts=ts, alpha=alpha)
        h2 = _ffn_ple(h2, p, w_fi, w_fo, vec(ln_ffn_g), vec(ln_ffn_b), w_pg, vec(b_ple_gate), w_p,
                      vec(ln_ple_g), vec(ln_ple_b), l, B=B, S=S, D=D, d_ff=d_ff, d_ple=d_ple,
                      ts=ts, alpha=alpha, batch_major_out=(l == L - 1))
    return h2
```

```python
import functools
import math

import jax
import jax.numpy as jnp
from jax import lax
from jax.experimental import pallas as pl
from jax.experimental.pallas import tpu as pltpu

LN_EPS = 1e-5
RG_C = 8.0
CONV_WIDTH = 4
LOG2E = 1.4426950408889634
SQRT_ARG_FLOOR = 1e-30
LANES = 128
SUBLANES = 8
KEY_TERM_SHIFT = 3
KEY_TERM_STRIDE = 1 << KEY_TERM_SHIFT
VMEM_LIMIT_BYTES = 56 * 1024 * 1024
ROW_TILE = 512
FFN_CHUNK_COLS = 1024

F32 = jnp.float32
BF16 = jnp.bfloat16


def _params(*semantics):
    return pltpu.CompilerParams(dimension_semantics=semantics,
                                vmem_limit_bytes=VMEM_LIMIT_BYTES)


def _layer_spec(tail, l):
    zeros = (0,) * len(tail)
    return pl.BlockSpec((None,) + tuple(tail), lambda *_: (l,) + zeros,
                        pipeline_mode=pl.Buffered(1))


def _const_spec(shape):
    zeros = (0,) * len(shape)
    return pl.BlockSpec(tuple(shape), lambda *_: zeros, pipeline_mode=pl.Buffered(1))


def _layer_norm(x, g, b):
    mu = jnp.mean(x, axis=-1, keepdims=True)
    xc = x - mu
    var = jnp.mean(xc * xc, axis=-1, keepdims=True)
    return xc * lax.rsqrt(var + LN_EPS) * g + b


def _split3(x):
    hi = x.astype(BF16)
    r1 = x - hi.astype(F32)
    mid = r1.astype(BF16)
    lo = (r1 - mid.astype(F32)).astype(BF16)
    return hi, mid, lo


def _ln_in_kernel(x_ref, g_ref, b_ref, o_ref):
    o_ref[...] = _layer_norm(x_ref[...], g_ref[...], b_ref[...])


def _ln_in(x, g, b, ts):
    B, S, D = x.shape
    return pl.pallas_call(
        _ln_in_kernel,
        grid=(B, S // ts),
        in_specs=[pl.BlockSpec((None, ts, D), lambda b_, i: (b_, i, 0)),
                  _const_spec((1, D)), _const_spec((1, D))],
        out_specs=pl.BlockSpec((ts, D), lambda b_, i: (i, b_)),
        out_shape=jax.ShapeDtypeStruct((S, B * D), F32),
        compiler_params=_params("parallel", "parallel"),
        name="ln_in",
    )(x, g.reshape(1, D), b.reshape(1, D))


def _qkv_kernel(h_ref, w_ref, bfc_ref, qkv_ref, ccol_ref, kf_ref, carry_c,
                *, n_heads, head_dim, q_scale):
    i = pl.program_id(1)

    @pl.when(i == 0)
    def _():
        carry_c[...] = jnp.zeros_like(carry_c)

    hb = h_ref[...].astype(BF16)
    tm = hb.shape[0]
    d_qkv = 3 * n_heads * head_dim
    fc = jnp.dot(hb, w_ref[:, d_qkv:], preferred_element_type=F32) + bfc_ref[...]
    z = jnp.dot(hb, w_ref[:, :d_qkv], preferred_element_type=F32)
    for j in range(3 * n_heads):
        blk = z[:, j * head_dim:(j + 1) * head_dim]
        if j < n_heads:
            blk = blk * q_scale
        qkv_ref[j] = blk.astype(BF16)

    def log2_forget(f):
        return (jnp.minimum(f, 0.0) - jnp.log1p(jnp.exp(-jnp.abs(f)))) * LOG2E

    row = lax.broadcasted_iota(jnp.int32, (tm, tm), 0)
    col = lax.broadcasted_iota(jnp.int32, (tm, tm), 1)

    lower = jnp.where(col <= row, 1.0, 0.0).astype(BF16)
    cc = jnp.dot(lower, jnp.concatenate(_split3(log2_forget(fc)), axis=1),
                 preferred_element_type=F32)
    cc = cc[:, :LANES] + cc[:, LANES:2 * LANES] + cc[:, 2 * LANES:] + carry_c[...]
    ccol_ref[...] = cc
    carry_c[...] = cc[tm - 1:tm, :]

    lane = lax.broadcasted_iota(jnp.int32, (tm, LANES), 1)
    pieces = [jnp.where(lane < n_heads, piece.astype(F32), 0.0) for piece in _split3(-cc)]
    kf = pieces[0]
    for term in (1, 2):
        kf = kf + pltpu.roll(pieces[term], term * KEY_TERM_STRIDE, 1)
    kf_ref[...] = kf.astype(BF16)


def _qkv_proj(h2, w_qkv, b_fc, l, *, B, S, D, n_heads, head_dim, tm):
    d_att = n_heads * head_dim
    kern = functools.partial(_qkv_kernel, n_heads=n_heads, head_dim=head_dim,
                             q_scale=LOG2E / math.sqrt(head_dim))
    return pl.pallas_call(
        kern,
        grid=(B, S // tm),
        in_specs=[pl.BlockSpec((tm, D), lambda b_, i: (i, b_)),
                  _layer_spec((D, 3 * d_att + LANES), l),
                  _layer_spec((1, LANES), l)],
        out_specs=[pl.BlockSpec((3 * n_heads, None, tm, head_dim), lambda b_, i: (0, b_, i, 0)),
                   pl.BlockSpec((None, tm, LANES), lambda b_, i: (b_, i, 0)),
                   pl.BlockSpec((None, tm, LANES), lambda b_, i: (b_, i, 0))],
        out_shape=[jax.ShapeDtypeStruct((3 * n_heads, B, S, head_dim), BF16),
                   jax.ShapeDtypeStruct((B, S, LANES), F32),
                   jax.ShapeDtypeStruct((B, S, LANES), BF16)],
        scratch_shapes=[pltpu.VMEM((1, LANES), F32)],
        compiler_params=_params("parallel", "arbitrary"),
        name="qkv_proj",
    )(h2, w_qkv, b_fc)


def _attn_kernel(q_ref, k_ref, v_ref, cq_ref, kf_ref, o_ref, *, n_heads, tq, group, wide):
    i = pl.program_id(1)
    n_q_tiles = kf_ref.shape[0] // tq
    lane = lax.broadcasted_iota(jnp.int32, (tq, LANES), 1)
    r_idx = lax.broadcasted_iota(jnp.int32, (tq, tq), 0)
    c_idx = lax.broadcasted_iota(jnp.int32, (tq, tq), 1)
    causal_bias = jnp.where(c_idx <= r_idx, 0.0, -jnp.inf)
    cq_all = cq_ref[...]
    nt = (((1,), (1,)), ((), ()))

    def chain_inputs(h):
        pick = jnp.where(lane < 3 * KEY_TERM_STRIDE,
                         jnp.where((lane & (KEY_TERM_STRIDE - 1)) == h, 1.0, 0.0), 0.0)
        q_aug = jnp.concatenate([q_ref[h], pick.astype(BF16)], axis=1)
        c_q = jnp.sum(jnp.where(lane == h, cq_all, 0.0), axis=1, keepdims=True)
        return q_aug, c_q

    def operands(h, start, width):
        keys = pl.ds(start, width)
        k_aug = jnp.concatenate([k_ref[h, keys, :], kf_ref[keys, :]], axis=1)
        v_aug = jnp.concatenate([v_ref[h, keys, :], jnp.ones((width, LANES), BF16)], axis=1)
        return k_aug, v_aug

    def advance(state, s, v_aug, c_q):
        row_max = c_q + jnp.max(s, axis=1, keepdims=True)
        if state is None:
            p = jnp.exp2(s + (c_q - row_max))
            return row_max, jnp.dot(p.astype(BF16), v_aug, preferred_element_type=F32)
        m, acc = state
        m_new = jnp.maximum(m, row_max)
        alpha = jnp.exp2(m - m_new)
        p = jnp.exp2(s + (c_q - m_new))
        return m_new, alpha * acc + jnp.dot(p.astype(BF16), v_aug, preferred_element_type=F32)

    def head_group(g, _, *, n_full):
        heads = [g * group + u for u in range(group)]
        inputs = [chain_inputs(h) for h in heads]
        state = [None] * group
        spans, j = [], 0
        while j < n_full:
            step = min(wide, n_full - j)
            spans.append((j * tq, step * tq, False))
            j += step
        spans.append((n_full * tq, tq, True))
        for start, width, diagonal in spans:
            for u, (h, (q_aug, c_q)) in enumerate(zip(heads, inputs)):
                k_aug, v_aug = operands(h, start, width)
                s = lax.dot_general(q_aug, k_aug, nt, preferred_element_type=F32)
                if diagonal:
                    s = s + causal_bias
                state[u] = advance(state[u], s, v_aug, c_q)
        for u, h in enumerate(heads):
            acc = state[u][1]
            o_ref[h] = (acc[:, :LANES] / acc[:, LANES:]).astype(o_ref.dtype)
        return 0

    for n_full in range(n_q_tiles):
        @pl.when(i == n_full)
        def _(n_full=n_full):
            lax.fori_loop(0, n_heads // group, functools.partial(head_group, n_full=n_full), 0)


def _attention(qkv, ccol, kf, *, B, S, n_heads, head_dim, tq, group):
    H = n_heads
    kern = functools.partial(_attn_kernel, n_heads=H, tq=tq, group=group, wide=3)
    return pl.pallas_call(
        kern,
        grid=(B, S // tq),
        in_specs=[pl.BlockSpec((H, None, tq, head_dim), lambda b_, i: (0, b_, i, 0)),
                  pl.BlockSpec((H, None, S, head_dim), lambda b_, i: (1, b_, 0, 0)),
                  pl.BlockSpec((H, None, S, head_dim), lambda b_, i: (2, b_, 0, 0)),
                  pl.BlockSpec((None, tq, LANES), lambda b_, i: (b_, i, 0)),
                  pl.BlockSpec((None, S, LANES), lambda b_, i: (b_, 0, 0))],
        out_specs=pl.BlockSpec((H, None, tq, head_dim), lambda b_, i: (0, b_, i, 0)),
        out_shape=jax.ShapeDtypeStruct((H, B, S, head_dim), BF16),
        compiler_params=_params("parallel", "arbitrary"),
        name="fox_attention",
    )(qkv, qkv, qkv, ccol, kf)


def _row_pitch(tt):
    groups = tt // SUBLANES + 1
    return SUBLANES * (groups if groups % 2 else groups + 1)


def _gelu_tanh(x):
    c = math.sqrt(2.0 / math.pi)
    half_x = 0.5 * x
    return half_x + half_x * jnp.tanh(x * (c + (c * 0.044715) * (x * x)))


def _rnn_kernel(h_ref, w_ref, cw_ref, cb_ref, wax_ref, ba_ref, bx_ref, lam_ref,
                o_ref, rx_even, rx_odd, gy_even, gy_odd, x_tb, h_bt, h_carry,
                *, n_batch, n_blocks, blk, pitch, n_tiles):
    i = pl.program_id(0)
    tt = h_ref.shape[0]
    rows, d = tt * n_batch, n_blocks * blk
    halo = (CONV_WIDTH - 1) * n_batch
    chunk = 2 * d // n_blocks

    @pl.when(i == 0)
    def _():
        x_tb[0:halo, :] = jnp.zeros((halo, d), F32)
        h_carry[...] = jnp.zeros_like(h_carry)
        rx_odd[...] = jnp.zeros_like(rx_odd)
        gy_odd[...] = jnp.zeros_like(gy_odd)

    def body(cur, prev, project):
        (rx_cur, gy_cur), (rx_prev, gy_prev) = cur, prev
        half_l2 = (-0.5 * RG_C * LOG2E) * (jnp.maximum(-lam_ref[...], 0.0)
                                           + jnp.log1p(jnp.exp(-jnp.abs(lam_ref[...]))))
        half_ba = 0.5 * ba_ref[...]
        half_bx = 0.5 * bx_ref[...]
        if project:
            hb = jnp.concatenate([h_ref[:, b * d:(b + 1) * d].astype(BF16) for b in range(n_batch)],
                                 axis=0)

        def conv_and_gate_logits(n):
            sl = slice(n * blk, (n + 1) * blk)
            for t in range(tt):
                x_tb[halo + t * n_batch:halo + (t + 1) * n_batch, sl] = (
                    rx_prev[n, pl.ds(t, n_batch, stride=pitch), :])
            half_xc = 0.5 * cb_ref[:, sl] + (0.5 * cw_ref[0:1, sl]) * x_tb[0:rows, sl]
            for k in range(1, CONV_WIDTH):
                half_xc = half_xc + (0.5 * cw_ref[k:k + 1, sl]) * x_tb[k * n_batch:k * n_batch + rows, sl]
            x_tb[0:halo, sl] = x_tb[rows:rows + halo, sl]
            return half_xc, jnp.dot(half_xc.astype(BF16), wax_ref[n], preferred_element_type=F32)

        staged = conv_and_gate_logits(0)
        for n in range(n_blocks):
            sl = slice(n * blk, (n + 1) * blk)
            half_x, g = staged
            if n + 1 < n_blocks:
                staged = conv_and_gate_logits(n + 1)
            t_r = jnp.tanh(g[:, :blk] + half_ba[:, sl])
            t_i = jnp.tanh(g[:, blk:] + half_bx[:, sl])
            a = jnp.exp2(half_l2[:, sl] + half_l2[:, sl] * t_r)
            one_m_a2 = 1.0 - a * a
            mult = one_m_a2 * lax.rsqrt(jnp.maximum(one_m_a2, SQRT_ARG_FLOOR))
            u = mult * (half_x + half_x * t_i)

            h = h_carry[:, sl]
            for t in range(tt):
                h = a[t * n_batch:(t + 1) * n_batch, :] * h + u[t * n_batch:(t + 1) * n_batch, :]
                h_bt[n, pl.ds(t, n_batch, stride=pitch), :] = h
            h_carry[:, sl] = jnp.where(i > 0, h, 0.0)

            for b in range(n_batch):
                o_ref[:, b * d + n * blk:b * d + (n + 1) * blk] = (
                    h_bt[n, b * pitch:b * pitch + tt, :] * gy_prev[b * tt:(b + 1) * tt, sl]
                ).astype(o_ref.dtype)

            if not project:
                continue
            zc = jnp.dot(hb, w_ref[:, n * chunk:(n + 1) * chunk], preferred_element_type=F32)
            for sub in range(chunk // blk):
                col = n * chunk + sub * blk
                piece = zc[:, sub * blk:(sub + 1) * blk]
                if col < d:
                    for b in range(n_batch):
                        rx_cur[col // blk, b * pitch:b * pitch + tt, :] = piece[b * tt:(b + 1) * tt, :]
                else:
                    gy_cur[:, col - d:col - d + blk] = _gelu_tanh(piece)

    _by_parity(i, n_tiles, body, (rx_even, gy_even), (rx_odd, gy_odd))


def _rnn_branch(h2, w_r, conv_w, conv_b, w_ax, b_a, b_x, lam, l, *, B, S, D, n_blocks, tt):
    rows = tt * B
    blk = D // n_blocks
    pitch = _row_pitch(tt)
    n_tiles = S // tt
    kern = functools.partial(_rnn_kernel, n_batch=B, n_blocks=n_blocks, blk=blk, pitch=pitch,
                             n_tiles=n_tiles)
    slabs = pltpu.VMEM((n_blocks, B * pitch, blk), F32)
    return pl.pallas_call(
        kern,
        grid=(n_tiles + 1,),
        in_specs=[pl.BlockSpec((tt, B * D), lambda i: (jnp.minimum(i, n_tiles - 1), 0)),
                  _layer_spec((D, 2 * D), l),
                  _layer_spec((CONV_WIDTH, D), l),
                  _layer_spec((1, D), l),
                  _layer_spec((n_blocks, blk, 2 * blk), l),
                  _layer_spec((1, D), l), _layer_spec((1, D), l), _layer_spec((1, D), l)],
        out_specs=pl.BlockSpec((tt, B * D), lambda i: (jnp.maximum(i - 1, 0), 0)),
        out_shape=jax.ShapeDtypeStruct((S, B * D), BF16),
        scratch_shapes=[slabs, slabs,
                        pltpu.VMEM((rows, D), F32), pltpu.VMEM((rows, D), F32),
                        pltpu.VMEM((rows + (CONV_WIDTH - 1) * B, D), F32),
                        slabs,
                        pltpu.VMEM((B, D), F32)],
        compiler_params=_params("arbitrary"),
        name="rglru_branch",
    )(h2, w_r, conv_w, conv_b, w_ax, b_a, b_x, lam)


def _by_parity(step, n_tiles, body, even, odd):
    parity = lax.rem(step, 2)
    roles = ((even, odd), (odd, even))
    for par, (cur, prev) in enumerate(roles):
        if par == n_tiles % 2:
            pl.when(jnp.logical_and(parity == par, step < n_tiles))(
                functools.partial(body, cur, prev, True))
            pl.when(step == n_tiles)(functools.partial(body, cur, prev, False))
        else:
            pl.when(parity == par)(functools.partial(body, cur, prev, True))


def _tile_maps(n_tiles, tiles_per_batch):
    def head(g):
        t = jnp.minimum(g, n_tiles - 1)
        return t % tiles_per_batch, t // tiles_per_batch

    def tail(g):
        t = jnp.maximum(g - 1, 0)
        return t % tiles_per_batch, t // tiles_per_batch
    return head, tail


def _merge_kernel(h_ref, att_ref, rnn_ref, wg_ref, wa_ref, wr_ref, wo_ref, bm_ref, g_ref, b_ref,
                  o_ref, y_even, y_odd, *, n_heads, alpha, n_tiles):
    step = pl.program_id(0)

    @pl.when(step == 0)
    def _():
        y_odd[...] = jnp.zeros_like(y_odd)

    def body(y_cur, y_prev, project):
        if project:
            h = h_ref[...]
            d = h.shape[1]
            gates = jnp.dot(h.astype(BF16), wg_ref[...], preferred_element_type=F32)
        o_ref[...] = _layer_norm(y_prev[...], g_ref[...], b_ref[...])
        if project:
            att = jnp.concatenate([att_ref[j] for j in range(n_heads)], axis=1)
            ya = jnp.dot(att, wa_ref[...], preferred_element_type=F32)
            yb = jnp.dot(rnn_ref[...], wr_ref[...], preferred_element_type=F32)
            t_a = jnp.tanh(gates[:, :d] + 0.5 * bm_ref[0:1, :])
            t_b = jnp.tanh(gates[:, d:] + 0.5 * bm_ref[1:2, :])
            merged = (ya + ya * t_a) + (yb + yb * t_b)
            m = jnp.dot(merged.astype(BF16), wo_ref[...], preferred_element_type=F32)
            y_cur[...] = alpha * h + m

    _by_parity(step, n_tiles, body, y_even, y_odd)


def _merge(h2, att, rnn2, w_g, w_a, w_r, w_o, b_m, ln_g, ln_b, l, *, B, S, D, n_heads, head_dim,
           ts, alpha):
    kern = functools.partial(_merge_kernel, n_heads=n_heads, alpha=alpha, n_tiles=B * (S // ts))
    d_att = n_heads * head_dim
    n_tiles = B * (S // ts)
    head, tail = _tile_maps(n_tiles, S // ts)
    tok = pl.BlockSpec((ts, D), lambda g: head(g))
    return pl.pallas_call(
        kern,
        grid=(n_tiles + 1,),
        in_specs=[tok,
                  pl.BlockSpec((n_heads, None, ts, head_dim),
                               lambda g: (0, head(g)[1], head(g)[0], 0)),
                  tok,
                  _layer_spec((D, 2 * D), l), _layer_spec((d_att, D), l),
                  _layer_spec((D, D), l), _layer_spec((D, D), l),
                  _layer_spec((2, D), l), _layer_spec((1, D), l), _layer_spec((1, D), l)],
        out_specs=pl.BlockSpec((ts, D), lambda g: tail(g)),
        out_shape=jax.ShapeDtypeStruct((S, B * D), F32),
        scratch_shapes=[pltpu.VMEM((ts, D), F32), pltpu.VMEM((ts, D), F32)],
        compiler_params=_params("arbitrary"),
        name="merge_out",
    )(h2, att, rnn2, w_g, w_a, w_r, w_o, b_m, ln_g, ln_b)


def _ffn_kernel(h_ref, p_ref, wi_ref, wo_ref, g1_ref, b1_ref, wpg_ref, bpg_ref, wp_ref,
                g2_ref, b2_ref, o_ref, y_even, y_odd, *, d_ff, chunks, alpha, n_tiles):
    step = pl.program_id(0)

    @pl.when(step == 0)
    def _():
        y_odd[...] = jnp.zeros_like(y_odd)

    def body(y_cur, y_prev, project):
        def previous_tile_inputs():
            h1 = _layer_norm(y_prev[...], g1_ref[...], b1_ref[...])
            return (h1, jnp.dot(h1.astype(BF16), wpg_ref[...], preferred_element_type=F32),
                    jnp.dot(p_ref[...].astype(BF16), wp_ref[...], preferred_element_type=F32))

        if not project:
            finish_previous_tile(*previous_tile_inputs())
            return
        h = h_ref[...]
        hb = h.astype(BF16)

        def up(lo, hi):
            return (jnp.dot(hb, wi_ref[:, lo:hi], preferred_element_type=F32),
                    jnp.dot(hb, wi_ref[:, d_ff + lo:d_ff + hi], preferred_element_type=F32))

        def down(half_g, hu, lo, hi):
            act = ((half_g + half_g * jnp.tanh(half_g)) * hu).astype(BF16)
            return jnp.dot(act, wo_ref[lo:hi, :], preferred_element_type=F32)

        f = down(*up(*chunks[0]), *chunks[0])
        tail_inputs = previous_tile_inputs()
        for lo, hi in chunks[1:2]:
            f = f + down(*up(lo, hi), lo, hi)
        finish_previous_tile(*tail_inputs)
        for lo, hi in chunks[2:]:
            f = f + down(*up(lo, hi), lo, hi)
        y_cur[...] = alpha * h + f

    def finish_previous_tile(h1, gate_logits, emb):
        gated = emb + emb * jnp.tanh(gate_logits + 0.5 * bpg_ref[...])
        o_ref[...] = _layer_norm(alpha * h1 + gated, g2_ref[...], b2_ref[...])

    _by_parity(step, n_tiles, body, y_even, y_odd)


def _ffn_chunks(d_ff, width):
    edges = list(range(0, d_ff, width)) + [d_ff]
    return tuple(zip(edges[:-1], edges[1:]))


def _ffn_ple(h2, p, w_i, w_o, g1, b1, w_pg, b_pg, w_p, g2, b2, l, *, B, S, D, d_ff, d_ple, ts,
             alpha, batch_major_out):
    n_tiles = B * (S // ts)
    kern = functools.partial(_ffn_kernel, d_ff=d_ff, chunks=_ffn_chunks(d_ff, FFN_CHUNK_COLS),
                             alpha=alpha, n_tiles=n_tiles)
    head, tail = _tile_maps(n_tiles, S // ts)
    if batch_major_out:
        out_spec = pl.BlockSpec((None, ts, D), lambda g: (tail(g)[1], tail(g)[0], 0))
        out_shape = jax.ShapeDtypeStruct((B, S, D), F32)
    else:
        out_spec = pl.BlockSpec((ts, D), lambda g: tail(g))
        out_shape = jax.ShapeDtypeStruct((S, B * D), F32)
    vec = _layer_spec((1, D), l)
    return pl.pallas_call(
        kern,
        grid=(n_tiles + 1,),
        in_specs=[pl.BlockSpec((ts, D), lambda g: head(g)),
                  pl.BlockSpec((None, None, ts, d_ple), lambda g: (l, tail(g)[1], tail(g)[0], 0)),
                  _layer_spec((D, 2 * d_ff), l), _layer_spec((d_ff, D), l), vec, vec,
                  _layer_spec((D, D), l), vec, _layer_spec((d_ple, D), l), vec, vec],
        out_specs=out_spec,
        out_shape=out_shape,
        scratch_shapes=[pltpu.VMEM((ts, D), F32), pltpu.VMEM((ts, D), F32)],
        compiler_params=_params("arbitrary"),
        name="ffn_ple",
    )(h2, p, w_i, w_o, g1, b1, w_pg, b_pg, w_p, g2, b2)


def kernel(x, p, ln_in_g, ln_in_b, w_in, b_forget, conv_w, conv_b, rg_w_a, rg_b_a, rg_w_x, rg_b_x,
           rg_lambda, w_branch_att, w_branch_rnn, b_merge, w_out, ln_mix_g, ln_mix_b, w_ffn_in,
           w_ffn_out, ln_ffn_g, ln_ffn_b, w_ple, w_ple_gate, b_ple_gate, ln_ple_g, ln_ple_b):
    B, S, D = x.shape
    L = w_in.shape[0]
    H = b_forget.shape[1]
    d_att = w_branch_att.shape[1]
    head_dim = d_att // H
    n_blocks, blk = rg_w_a.shape[1], rg_w_a.shape[2]
    d_ff = w_ffn_out.shape[1]
    d_ple = w_ple.shape[1]
    assert head_dim == LANES and blk == LANES and D == n_blocks * blk and d_att == D
    assert H <= KEY_TERM_STRIDE and B % SUBLANES == 0
    alpha = float((2 * L) ** 0.25)

    ts = min(ROW_TILE, S)
    tq = min(ROW_TILE, S)
    tt = max(1, min(S, ROW_TILE // B))

    o_f, o_rx = 3 * d_att, 3 * d_att + H
    o_g = o_rx + 2 * D
    w_qkv = jnp.pad(w_in[:, :, :o_rx], ((0, 0), (0, 0), (0, LANES - H))).astype(BF16)
    b_fc = jnp.pad(b_forget, ((0, 0), (0, LANES - H))).reshape(L, 1, LANES)
    w_r = w_in[:, :, o_rx:o_g].astype(BF16)
    w_g = (0.5 * w_in[:, :, o_g:]).astype(BF16)
    w_ax = jnp.concatenate([rg_w_a, rg_w_x], axis=-1).astype(BF16)
    w_ba, w_br = (0.5 * w_branch_att).astype(BF16), (0.5 * w_branch_rnn).astype(BF16)
    w_o = w_out.astype(BF16)
    gate_half = jnp.where(jnp.arange(2 * d_ff) < d_ff, 0.5, 1.0).astype(F32)
    w_fi, w_fo = (w_ffn_in * gate_half).astype(BF16), w_ffn_out.astype(BF16)
    w_pg, w_p = (0.5 * w_ple_gate).astype(BF16), (0.5 * w_ple).astype(BF16)
    vec = lambda a: a.reshape(L, 1, D)

    h2 = _ln_in(x, ln_in_g, ln_in_b, min(2 * ROW_TILE, S))
    for l in range(L):
        qkv, ccol, kf = _qkv_proj(h2, w_qkv, b_fc, l, B=B, S=S, D=D,
                                  n_heads=H, head_dim=head_dim, tm=ts)
        att = _attention(qkv, ccol, kf, B=B, S=S, n_heads=H, head_dim=head_dim, tq=tq, group=H)
        rnn = _rnn_branch(h2, w_r, conv_w, vec(conv_b), w_ax, vec(rg_b_a),
                          vec(rg_b_x), vec(rg_lambda), l, B=B, S=S, D=D, n_blocks=n_blocks, tt=tt)
        h2 = _merge(h2, att, rnn, w_g, w_ba, w_br, w_o, b_merge,
                    vec(ln_mix_g), vec(ln_mix_b), l, B=B, S=S, D=D, n_heads=H, head_dim=head_dim,
                    ts=ts, alpha=alpha)
        h2 = _ffn_ple(h2, p, w_fi, w_fo, vec(ln_ffn_g), vec(ln_ffn_b), w_pg, vec(b_ple_gate), w_p,
                      vec(ln_ple_g), vec(ln_ple_b), l, B=B, S=S, D=D, d_ff=d_ff, d_ple=d_ple,
                      ts=ts, alpha=alpha, batch_major_out=(l == L - 1))
    return h2
```

```python
import functools
import math

import jax
import jax.numpy as jnp
from jax import lax
from jax.experimental import pallas as pl
from jax.experimental.pallas import tpu as pltpu

LN_EPS = 1e-5
RG_C = 8.0
CONV_WIDTH = 4
LOG2E = 1.4426950408889634
SQRT_ARG_FLOOR = 1e-30
LANES = 128
SUBLANES = 8
KEY_TERM_SHIFT = 3
KEY_TERM_STRIDE = 1 << KEY_TERM_SHIFT
VMEM_LIMIT_BYTES = 56 * 1024 * 1024
ROW_TILE = 512
FFN_CHUNK_COLS = 1024

F32 = jnp.float32
BF16 = jnp.bfloat16


def _params(*semantics):
    return pltpu.CompilerParams(dimension_semantics=semantics,
                                vmem_limit_bytes=VMEM_LIMIT_BYTES)


def _layer_spec(tail, l):
    zeros = (0,) * len(tail)
    return pl.BlockSpec((None,) + tuple(tail), lambda *_: (l,) + zeros,
                        pipeline_mode=pl.Buffered(1))


def _const_spec(shape):
    zeros = (0,) * len(shape)
    return pl.BlockSpec(tuple(shape), lambda *_: zeros, pipeline_mode=pl.Buffered(1))


def _sigmoid(x):
    return 1.0 / (1.0 + jnp.exp(-x))


def _layer_norm(x, g, b):
    mu = jnp.mean(x, axis=-1, keepdims=True)
    xc = x - mu
    var = jnp.mean(xc * xc, axis=-1, keepdims=True)
    return xc * lax.rsqrt(var + LN_EPS) * g + b


def _split3(x):
    hi = x.astype(BF16)
    r1 = x - hi.astype(F32)
    mid = r1.astype(BF16)
    lo = (r1 - mid.astype(F32)).astype(BF16)
    return hi, mid, lo


def _ln_in_kernel(x_ref, g_ref, b_ref, o_ref):
    o_ref[...] = _layer_norm(x_ref[...], g_ref[...], b_ref[...])


def _ln_in(x, g, b, ts):
    B, S, D = x.shape
    return pl.pallas_call(
        _ln_in_kernel,
        grid=(B, S // ts),
        in_specs=[pl.BlockSpec((None, ts, D), lambda b_, i: (b_, i, 0)),
                  _const_spec((1, D)), _const_spec((1, D))],
        out_specs=pl.BlockSpec((ts, D), lambda b_, i: (i, b_)),
        out_shape=jax.ShapeDtypeStruct((S, B * D), F32),
        compiler_params=_params("parallel", "parallel"),
        name="ln_in",
    )(x, g.reshape(1, D), b.reshape(1, D))


def _qkv_kernel(h_ref, w_ref, bfc_ref, qkv_ref, ccol_ref, kf_ref, carry_c,
                *, n_heads, head_dim, q_scale):
    i = pl.program_id(1)

    @pl.when(i == 0)
    def _():
        carry_c[...] = jnp.zeros_like(carry_c)

    hb = h_ref[...].astype(BF16)
    tm = hb.shape[0]
    d_qkv = 3 * n_heads * head_dim
    fc = jnp.dot(hb, w_ref[:, d_qkv:], preferred_element_type=F32) + bfc_ref[...]
    z = jnp.dot(hb, w_ref[:, :d_qkv], preferred_element_type=F32)
    for j in range(3 * n_heads):
        blk = z[:, j * head_dim:(j + 1) * head_dim]
        if j < n_heads:
            blk = blk * q_scale
        qkv_ref[j] = blk.astype(BF16)

    def log2_forget(f):
        return (jnp.minimum(f, 0.0) - jnp.log1p(jnp.exp(-jnp.abs(f)))) * LOG2E

    row = lax.broadcasted_iota(jnp.int32, (tm, tm), 0)
    col = lax.broadcasted_iota(jnp.int32, (tm, tm), 1)

    lower = jnp.where(col <= row, 1.0, 0.0).astype(BF16)
    cc = jnp.dot(lower, jnp.concatenate(_split3(log2_forget(fc)), axis=1),
                 preferred_element_type=F32)
    cc = cc[:, :LANES] + cc[:, LANES:2 * LANES] + cc[:, 2 * LANES:] + carry_c[...]
    ccol_ref[...] = cc
    carry_c[...] = cc[tm - 1:tm, :]

    lane = lax.broadcasted_iota(jnp.int32, (tm, LANES), 1)
    pieces = [jnp.where(lane < n_heads, piece.astype(F32), 0.0) for piece in _split3(-cc)]
    kf = pieces[0]
    for term in (1, 2):
        kf = kf + pltpu.roll(pieces[term], term * KEY_TERM_STRIDE, 1)
    kf_ref[...] = kf.astype(BF16)


def _qkv_proj(h2, w_qkv, b_fc, l, *, B, S, D, n_heads, head_dim, tm):
    d_att = n_heads * head_dim
    kern = functools.partial(_qkv_kernel, n_heads=n_heads, head_dim=head_dim,
                             q_scale=LOG2E / math.sqrt(head_dim))
    return pl.pallas_call(
        kern,
        grid=(B, S // tm),
        in_specs=[pl.BlockSpec((tm, D), lambda b_, i: (i, b_)),
                  _layer_spec((D, 3 * d_att + LANES), l),
                  _layer_spec((1, LANES), l)],
        out_specs=[pl.BlockSpec((3 * n_heads, None, tm, head_dim), lambda b_, i: (0, b_, i, 0)),
                   pl.BlockSpec((None, tm, LANES), lambda b_, i: (b_, i, 0)),
                   pl.BlockSpec((None, tm, LANES), lambda b_, i: (b_, i, 0))],
        out_shape=[jax.ShapeDtypeStruct((3 * n_heads, B, S, head_dim), BF16),
                   jax.ShapeDtypeStruct((B, S, LANES), F32),
                   jax.ShapeDtypeStruct((B, S, LANES), BF16)],
        scratch_shapes=[pltpu.VMEM((1, LANES), F32)],
        compiler_params=_params("parallel", "arbitrary"),
        name="qkv_proj",
    )(h2, w_qkv, b_fc)


def _attn_kernel(q_ref, k_ref, v_ref, cq_ref, kf_ref, o_ref, *, n_heads, tq, group, wide):
    i = pl.program_id(1)
    n_q_tiles = kf_ref.shape[0] // tq
    lane = lax.broadcasted_iota(jnp.int32, (tq, LANES), 1)
    r_idx = lax.broadcasted_iota(jnp.int32, (tq, tq), 0)
    c_idx = lax.broadcasted_iota(jnp.int32, (tq, tq), 1)
    causal_bias = jnp.where(c_idx <= r_idx, 0.0, -jnp.inf)
    cq_all = cq_ref[...]
    nt = (((1,), (1,)), ((), ()))

    def chain_inputs(h):
        pick = jnp.where(lane < 3 * KEY_TERM_STRIDE,
                         jnp.where((lane & (KEY_TERM_STRIDE - 1)) == h, 1.0, 0.0), 0.0)
        q_aug = jnp.concatenate([q_ref[h], pick.astype(BF16)], axis=1)
        c_q = jnp.sum(jnp.where(lane == h, cq_all, 0.0), axis=1, keepdims=True)
        return q_aug, c_q

    def operands(h, start, width):
        keys = pl.ds(start, width)
        k_aug = jnp.concatenate([k_ref[h, keys, :], kf_ref[keys, :]], axis=1)
        v_aug = jnp.concatenate([v_ref[h, keys, :], jnp.ones((width, LANES), BF16)], axis=1)
        return k_aug, v_aug

    def advance(state, s, v_aug, c_q):
        row_max = c_q + jnp.max(s, axis=1, keepdims=True)
        if state is None:
            p = jnp.exp2(s + (c_q - row_max))
            return row_max, jnp.dot(p.astype(BF16), v_aug, preferred_element_type=F32)
        m, acc = state
        m_new = jnp.maximum(m, row_max)
        alpha = jnp.exp2(m - m_new)
        p = jnp.exp2(s + (c_q - m_new))
        return m_new, alpha * acc + jnp.dot(p.astype(BF16), v_aug, preferred_element_type=F32)

    def head_group(g, _, *, n_full):
        heads = [g * group + u for u in range(group)]
        inputs = [chain_inputs(h) for h in heads]
        state = [None] * group
        spans, j = [], 0
        while j < n_full:
            step = min(wide, n_full - j)
            spans.append((j * tq, step * tq, False))
            j += step
        spans.append((n_full * tq, tq, True))
        for start, width, diagonal in spans:
            for u, (h, (q_aug, c_q)) in enumerate(zip(heads, inputs)):
                k_aug, v_aug = operands(h, start, width)
                s = lax.dot_general(q_aug, k_aug, nt, preferred_element_type=F32)
                if diagonal:
                    s = s + causal_bias
                state[u] = advance(state[u], s, v_aug, c_q)
        for u, h in enumerate(heads):
            acc = state[u][1]
            o_ref[h] = (acc[:, :LANES] / acc[:, LANES:]).astype(o_ref.dtype)
        return 0

    for n_full in range(n_q_tiles):
        @pl.when(i == n_full)
        def _(n_full=n_full):
            lax.fori_loop(0, n_heads // group, functools.partial(head_group, n_full=n_full), 0)


def _attention(qkv, ccol, kf, *, B, S, n_heads, head_dim, tq, group):
    H = n_heads
    kern = functools.partial(_attn_kernel, n_heads=H, tq=tq, group=group, wide=3)
    return pl.pallas_call(
        kern,
        grid=(B, S // tq),
        in_specs=[pl.BlockSpec((H, None, tq, head_dim), lambda b_, i: (0, b_, i, 0)),
                  pl.BlockSpec((H, None, S, head_dim), lambda b_, i: (1, b_, 0, 0)),
                  pl.BlockSpec((H, None, S, head_dim), lambda b_, i: (2, b_, 0, 0)),
                  pl.BlockSpec((None, tq, LANES), lambda b_, i: (b_, i, 0)),
                  pl.BlockSpec((None, S, LANES), lambda b_, i: (b_, 0, 0))],
        out_specs=pl.BlockSpec((H, None, tq, head_dim), lambda b_, i: (0, b_, i, 0)),
        out_shape=jax.ShapeDtypeStruct((H, B, S, head_dim), BF16),
        compiler_params=_params("parallel", "arbitrary"),
        name="fox_attention",
    )(qkv, qkv, qkv, ccol, kf)


def _row_pitch(tt):
    groups = tt // SUBLANES + 1
    return SUBLANES * (groups if groups % 2 else groups + 1)


def _gelu_tanh(x):
    c = math.sqrt(2.0 / math.pi)
    half_x = 0.5 * x
    return half_x + half_x * jnp.tanh(x * (c + (c * 0.044715) * (x * x)))


def _rnn_kernel(h_ref, w_ref, cw_ref, cb_ref, wax_ref, ba_ref, bx_ref, lam_ref,
                o_ref, rx_even, rx_odd, gy_even, gy_odd, x_tb, h_bt, h_carry,
                *, n_batch, n_blocks, blk, pitch, n_tiles):
    i = pl.program_id(0)
    tt = h_ref.shape[0]
    rows, d = tt * n_batch, n_blocks * blk
    halo = (CONV_WIDTH - 1) * n_batch
    chunk = 2 * d // n_blocks

    @pl.when(i == 0)
    def _():
        x_tb[0:halo, :] = jnp.zeros((halo, d), F32)
        h_carry[...] = jnp.zeros_like(h_carry)
        rx_odd[...] = jnp.zeros_like(rx_odd)
        gy_odd[...] = jnp.zeros_like(gy_odd)

    def body(cur, prev, project):
        (rx_cur, gy_cur), (rx_prev, gy_prev) = cur, prev
        half_l2 = (-0.5 * RG_C * LOG2E) * (jnp.maximum(-lam_ref[...], 0.0)
                                           + jnp.log1p(jnp.exp(-jnp.abs(lam_ref[...]))))
        half_ba = 0.5 * ba_ref[...]
        half_bx = 0.5 * bx_ref[...]
        if project:
            hb = jnp.concatenate([h_ref[:, b * d:(b + 1) * d].astype(BF16) for b in range(n_batch)],
                                 axis=0)

        def conv_and_gate_logits(n):
            sl = slice(n * blk, (n + 1) * blk)
            for t in range(tt):
                x_tb[halo + t * n_batch:halo + (t + 1) * n_batch, sl] = (
                    rx_prev[n, pl.ds(t, n_batch, stride=pitch), :])
            half_xc = 0.5 * cb_ref[:, sl] + (0.5 * cw_ref[0:1, sl]) * x_tb[0:rows, sl]
            for k in range(1, CONV_WIDTH):
                half_xc = half_xc + (0.5 * cw_ref[k:k + 1, sl]) * x_tb[k * n_batch:k * n_batch + rows, sl]
            x_tb[0:halo, sl] = x_tb[rows:rows + halo, sl]
            return half_xc, jnp.dot(half_xc.astype(BF16), wax_ref[n], preferred_element_type=F32)

        staged = conv_and_gate_logits(0)
        for n in range(n_blocks):
            sl = slice(n * blk, (n + 1) * blk)
            half_x, g = staged
            if n + 1 < n_blocks:
                staged = conv_and_gate_logits(n + 1)
            t_r = jnp.tanh(g[:, :blk] + half_ba[:, sl])
            t_i = jnp.tanh(g[:, blk:] + half_bx[:, sl])
            a = jnp.exp2(half_l2[:, sl] + half_l2[:, sl] * t_r)
            one_m_a2 = 1.0 - a * a
            mult = one_m_a2 * lax.rsqrt(jnp.maximum(one_m_a2, SQRT_ARG_FLOOR))
            u = mult * (half_x + half_x * t_i)

            h = h_carry[:, sl]
            for t in range(tt):
                h = a[t * n_batch:(t + 1) * n_batch, :] * h + u[t * n_batch:(t + 1) * n_batch, :]
                h_bt[n, pl.ds(t, n_batch, stride=pitch), :] = h
            h_carry[:, sl] = jnp.where(i > 0, h, 0.0)

            for b in range(n_batch):
                o_ref[:, b * d + n * blk:b * d + (n + 1) * blk] = (
                    h_bt[n, b * pitch:b * pitch + tt, :] * gy_prev[b * tt:(b + 1) * tt, sl]
                ).astype(o_ref.dtype)

            if not project:
                continue
            zc = jnp.dot(hb, w_ref[:, n * chunk:(n + 1) * chunk], preferred_element_type=F32)
            for sub in range(chunk // blk):
                col = n * chunk + sub * blk
                piece = zc[:, sub * blk:(sub + 1) * blk]
                if col < d:
                    for b in range(n_batch):
                        rx_cur[col // blk, b * pitch:b * pitch + tt, :] = piece[b * tt:(b + 1) * tt, :]
                else:
                    gy_cur[:, col - d:col - d + blk] = _gelu_tanh(piece)

    _by_parity(i, n_tiles, body, (rx_even, gy_even), (rx_odd, gy_odd))


def _rnn_branch(h2, w_r, conv_w, conv_b, w_ax, b_a, b_x, lam, l, *, B, S, D, n_blocks, tt):
    rows = tt * B
    blk = D // n_blocks
    pitch = _row_pitch(tt)
    n_tiles = S // tt
    kern = functools.partial(_rnn_kernel, n_batch=B, n_blocks=n_blocks, blk=blk, pitch=pitch,
                             n_tiles=n_tiles)
    slabs = pltpu.VMEM((n_blocks, B * pitch, blk), F32)
    return pl.pallas_call(
        kern,
        grid=(n_tiles + 1,),
        in_specs=[pl.BlockSpec((tt, B * D), lambda i: (jnp.minimum(i, n_tiles - 1), 0)),
                  _layer_spec((D, 2 * D), l),
                  _layer_spec((CONV_WIDTH, D), l),
                  _layer_spec((1, D), l),
                  _layer_spec((n_blocks, blk, 2 * blk), l),
                  _layer_spec((1, D), l), _layer_spec((1, D), l), _layer_spec((1, D), l)],
        out_specs=pl.BlockSpec((tt, B * D), lambda i: (jnp.maximum(i - 1, 0), 0)),
        out_shape=jax.ShapeDtypeStruct((S, B * D), BF16),
        scratch_shapes=[slabs, slabs,
                        pltpu.VMEM((rows, D), F32), pltpu.VMEM((rows, D), F32),
                        pltpu.VMEM((rows + (CONV_WIDTH - 1) * B, D), F32),
                        slabs,
                        pltpu.VMEM((B, D), F32)],
        compiler_params=_params("arbitrary"),
        name="rglru_branch",
    )(h2, w_r, conv_w, conv_b, w_ax, b_a, b_x, lam)


def _by_parity(step, n_tiles, body, even, odd):
    parity = lax.rem(step, 2)
    roles = ((even, odd), (odd, even))
    for par, (cur, prev) in enumerate(roles):
        if par == n_tiles % 2:
            pl.when(jnp.logical_and(parity == par, step < n_tiles))(
                functools.partial(body, cur, prev, True))
            pl.when(step == n_tiles)(functools.partial(body, cur, prev, False))
        else:
            pl.when(parity == par)(functools.partial(body, cur, prev, True))


def _tile_maps(n_tiles, tiles_per_batch):
    def head(g):
        t = jnp.minimum(g, n_tiles - 1)
        return t % tiles_per_batch, t // tiles_per_batch

    def tail(g):
        t = jnp.maximum(g - 1, 0)
        return t % tiles_per_batch, t // tiles_per_batch
    return head, tail


def _merge_kernel(h_ref, att_ref, rnn_ref, wg_ref, wa_ref, wr_ref, wo_ref, bm_ref, g_ref, b_ref,
                  o_ref, y_even, y_odd, *, n_heads, alpha, n_tiles):
    step = pl.program_id(0)

    @pl.when(step == 0)
    def _():
        y_odd[...] = jnp.zeros_like(y_odd)

    def body(y_cur, y_prev, project):
        if project:
            h = h_ref[...]
            d = h.shape[1]
            gates = jnp.dot(h.astype(BF16), wg_ref[...], preferred_element_type=F32)
        o_ref[...] = _layer_norm(y_prev[...], g_ref[...], b_ref[...])
        if project:
            att = jnp.concatenate([att_ref[j] for j in range(n_heads)], axis=1)
            ya = jnp.dot(att, wa_ref[...], preferred_element_type=F32)
            yb = jnp.dot(rnn_ref[...], wr_ref[...], preferred_element_type=F32)
            merged = (_sigmoid(gates[:, :d] + bm_ref[0:1, :]) * ya
                      + _sigmoid(gates[:, d:] + bm_ref[1:2, :]) * yb)
            m = jnp.dot(merged.astype(BF16), wo_ref[...], preferred_element_type=F32)
            y_cur[...] = alpha * h + m

    _by_parity(step, n_tiles, body, y_even, y_odd)


def _merge(h2, att, rnn2, w_g, w_a, w_r, w_o, b_m, ln_g, ln_b, l, *, B, S, D, n_heads, head_dim,
           ts, alpha):
    kern = functools.partial(_merge_kernel, n_heads=n_heads, alpha=alpha, n_tiles=B * (S // ts))
    d_att = n_heads * head_dim
    n_tiles = B * (S // ts)
    head, tail = _tile_maps(n_tiles, S // ts)
    tok = pl.BlockSpec((ts, D), lambda g: head(g))
    return pl.pallas_call(
        kern,
        grid=(n_tiles + 1,),
        in_specs=[tok,
                  pl.BlockSpec((n_heads, None, ts, head_dim),
                               lambda g: (0, head(g)[1], head(g)[0], 0)),
                  tok,
                  _layer_spec((D, 2 * D), l), _layer_spec((d_att, D), l),
                  _layer_spec((D, D), l), _layer_spec((D, D), l),
                  _layer_spec((2, D), l), _layer_spec((1, D), l), _layer_spec((1, D), l)],
        out_specs=pl.BlockSpec((ts, D), lambda g: tail(g)),
        out_shape=jax.ShapeDtypeStruct((S, B * D), F32),
        scratch_shapes=[pltpu.VMEM((ts, D), F32), pltpu.VMEM((ts, D), F32)],
        compiler_params=_params("arbitrary"),
        name="merge_out",
    )(h2, att, rnn2, w_g, w_a, w_r, w_o, b_m, ln_g, ln_b)


def _ffn_kernel(h_ref, p_ref, wi_ref, wo_ref, g1_ref, b1_ref, wpg_ref, bpg_ref, wp_ref,
                g2_ref, b2_ref, o_ref, y_even, y_odd, *, d_ff, chunks, alpha, n_tiles):
    step = pl.program_id(0)

    @pl.when(step == 0)
    def _():
        y_odd[...] = jnp.zeros_like(y_odd)

    def body(y_cur, y_prev, project):
        def previous_tile_inputs():
            h1 = _layer_norm(y_prev[...], g1_ref[...], b1_ref[...])
            return (h1, jnp.dot(h1.astype(BF16), wpg_ref[...], preferred_element_type=F32),
                    jnp.dot(p_ref[...].astype(BF16), wp_ref[...], preferred_element_type=F32))

        if not project:
            finish_previous_tile(*previous_tile_inputs())
            return
        h = h_ref[...]
        hb = h.astype(BF16)

        def up(lo, hi):
            return (jnp.dot(hb, wi_ref[:, lo:hi], preferred_element_type=F32),
                    jnp.dot(hb, wi_ref[:, d_ff + lo:d_ff + hi], preferred_element_type=F32))

        def down(half_g, hu, lo, hi):
            act = ((half_g + half_g * jnp.tanh(half_g)) * hu).astype(BF16)
            return jnp.dot(act, wo_ref[lo:hi, :], preferred_element_type=F32)

        f = down(*up(*chunks[0]), *chunks[0])
        tail_inputs = previous_tile_inputs()
        for lo, hi in chunks[1:2]:
            f = f + down(*up(lo, hi), lo, hi)
        finish_previous_tile(*tail_inputs)
        for lo, hi in chunks[2:]:
            f = f + down(*up(lo, hi), lo, hi)
        y_cur[...] = alpha * h + f

    def finish_previous_tile(h1, gate_logits, emb):
        gated = emb + emb * jnp.tanh(gate_logits + 0.5 * bpg_ref[...])
        o_ref[...] = _layer_norm(alpha * h1 + gated, g2_ref[...], b2_ref[...])

    _by_parity(step, n_tiles, body, y_even, y_odd)


def _ffn_chunks(d_ff, width):
    edges = list(range(0, d_ff, width)) + [d_ff]
    return tuple(zip(edges[:-1], edges[1:]))


def _ffn_ple(h2, p, w_i, w_o, g1, b1, w_pg, b_pg, w_p, g2, b2, l, *, B, S, D, d_ff, d_ple, ts,
             alpha, batch_major_out):
    n_tiles = B * (S // ts)
    kern = functools.partial(_ffn_kernel, d_ff=d_ff, chunks=_ffn_chunks(d_ff, FFN_CHUNK_COLS),
                             alpha=alpha, n_tiles=n_tiles)
    head, tail = _tile_maps(n_tiles, S // ts)
    if batch_major_out:
        out_spec = pl.BlockSpec((None, ts, D), lambda g: (tail(g)[1], tail(g)[0], 0))
        out_shape = jax.ShapeDtypeStruct((B, S, D), F32)
    else:
        out_spec = pl.BlockSpec((ts, D), lambda g: tail(g))
        out_shape = jax.ShapeDtypeStruct((S, B * D), F32)
    vec = _layer_spec((1, D), l)
    return pl.pallas_call(
        kern,
        grid=(n_tiles + 1,),
        in_specs=[pl.BlockSpec((ts, D), lambda g: head(g)),
                  pl.BlockSpec((None, None, ts, d_ple), lambda g: (l, tail(g)[1], tail(g)[0], 0)),
                  _layer_spec((D, 2 * d_ff), l), _layer_spec((d_ff, D), l), vec, vec,
                  _layer_spec((D, D), l), vec, _layer_spec((d_ple, D), l), vec, vec],
        out_specs=out_spec,
        out_shape=out_shape,
        scratch_shapes=[pltpu.VMEM((ts, D), F32), pltpu.VMEM((ts, D), F32)],
        compiler_params=_params("arbitrary"),
        name="ffn_ple",
    )(h2, p, w_i, w_o, g1, b1, w_pg, b_pg, w_p, g2, b2)


def kernel(x, p, ln_in_g, ln_in_b, w_in, b_forget, conv_w, conv_b, rg_w_a, rg_b_a, rg_w_x, rg_b_x,
           rg_lambda, w_branch_att, w_branch_rnn, b_merge, w_out, ln_mix_g, ln_mix_b, w_ffn_in,
           w_ffn_out, ln_ffn_g, ln_ffn_b, w_ple, w_ple_gate, b_ple_gate, ln_ple_g, ln_ple_b):
    B, S, D = x.shape
    L = w_in.shape[0]
    H = b_forget.shape[1]
    d_att = w_branch_att.shape[1]
    head_dim = d_att // H
    n_blocks, blk = rg_w_a.shape[1], rg_w_a.shape[2]
    d_ff = w_ffn_out.shape[1]
    d_ple = w_ple.shape[1]
    assert head_dim == LANES and blk == LANES and D == n_blocks * blk and d_att == D
    assert H <= KEY_TERM_STRIDE and B % SUBLANES == 0
    alpha = float((2 * L) ** 0.25)

    ts = min(ROW_TILE, S)
    tq = min(ROW_TILE, S)
    tt = max(1, min(S, ROW_TILE // B))

    o_f, o_rx = 3 * d_att, 3 * d_att + H
    o_g = o_rx + 2 * D
    w_qkv = jnp.pad(w_in[:, :, :o_rx], ((0, 0), (0, 0), (0, LANES - H))).astype(BF16)
    b_fc = jnp.pad(b_forget, ((0, 0), (0, LANES - H))).reshape(L, 1, LANES)
    w_r = w_in[:, :, o_rx:o_g].astype(BF16)
    w_g = w_in[:, :, o_g:].astype(BF16)
    w_ax = jnp.concatenate([rg_w_a, rg_w_x], axis=-1).astype(BF16)
    w_ba, w_br, w_o = (w.astype(BF16) for w in (w_branch_att, w_branch_rnn, w_out))
    gate_half = jnp.where(jnp.arange(2 * d_ff) < d_ff, 0.5, 1.0).astype(F32)
    w_fi, w_fo = (w_ffn_in * gate_half).astype(BF16), w_ffn_out.astype(BF16)
    w_pg, w_p = (0.5 * w_ple_gate).astype(BF16), (0.5 * w_ple).astype(BF16)
    vec = lambda a: a.reshape(L, 1, D)

    h2 = _ln_in(x, ln_in_g, ln_in_b, min(2 * ROW_TILE, S))
    for l in range(L):
        qkv, ccol, kf = _qkv_proj(h2, w_qkv, b_fc, l, B=B, S=S, D=D,
                                  n_heads=H, head_dim=head_dim, tm=ts)
        att = _attention(qkv, ccol, kf, B=B, S=S, n_heads=H, head_dim=head_dim, tq=tq, group=H)
        rnn = _rnn_branch(h2, w_r, conv_w, vec(conv_b), w_ax, vec(rg_b_a),
                          vec(rg_b_x), vec(rg_lambda), l, B=B, S=S, D=D, n_blocks=n_blocks, tt=tt)
        h2 = _merge(h2, att, rnn, w_g, w_ba, w_br, w_o, b_merge,
                    vec(ln_mix_g), vec(ln_mix_b), l, B=B, S=S, D=D, n_heads=H, head_dim=head_dim,
                    ts=ts, alpha=alpha)
        h2 = _ffn_ple(h2, p, w_fi, w_fo, vec(ln_ffn_g), vec(ln_ffn_b), w_pg, vec(b_ple_gate), w_p,
                      vec(ln_ple_g), vec(ln_ple_b), l, B=B, S=S, D=D, d_ff=d_ff, d_ple=d_ple,
                      ts=ts, alpha=alpha, batch_major_out=(l == L - 1))
    return h2
```

```python
import functools
import math

import jax
import jax.numpy as jnp
from jax import lax
from jax.experimental import pallas as pl
from jax.experimental.pallas import tpu as pltpu

LN_EPS = 1e-5
RG_C = 8.0
CONV_WIDTH = 4
LOG2E = 1.4426950408889634
SQRT_ARG_FLOOR = 1e-30
LANES = 128
SUBLANES = 8
KEY_TERM_SHIFT = 3
KEY_TERM_STRIDE = 1 << KEY_TERM_SHIFT
VMEM_LIMIT_BYTES = 56 * 1024 * 1024
ROW_TILE = 512
FFN_CHUNK_COLS = 1024

F32 = jnp.float32
BF16 = jnp.bfloat16


def _params(*semantics):
    return pltpu.CompilerParams(dimension_semantics=semantics,
                                vmem_limit_bytes=VMEM_LIMIT_BYTES)


def _layer_spec(tail, l):
    zeros = (0,) * len(tail)
    return pl.BlockSpec((None,) + tuple(tail), lambda *_: (l,) + zeros,
                        pipeline_mode=pl.Buffered(1))


def _const_spec(shape):
    zeros = (0,) * len(shape)
    return pl.BlockSpec(tuple(shape), lambda *_: zeros, pipeline_mode=pl.Buffered(1))


def _layer_norm(x, g, b):
    mu = jnp.mean(x, axis=-1, keepdims=True)
    xc = x - mu
    var = jnp.mean(xc * xc, axis=-1, keepdims=True)
    return xc * lax.rsqrt(var + LN_EPS) * g + b


def _split3(x):
    hi = x.astype(BF16)
    r1 = x - hi.astype(F32)
    mid = r1.astype(BF16)
    lo = (r1 - mid.astype(F32)).astype(BF16)
    return hi, mid, lo


def _ln_in_kernel(x_ref, g_ref, b_ref, o_ref):
    o_ref[...] = _layer_norm(x_ref[...], g_ref[...], b_ref[...])


def _ln_in(x, g, b, ts):
    B, S, D = x.shape
    return pl.pallas_call(
        _ln_in_kernel,
        grid=(B, S // ts),
        in_specs=[pl.BlockSpec((None, ts, D), lambda b_, i: (b_, i, 0)),
                  _const_spec((1, D)), _const_spec((1, D))],
        out_specs=pl.BlockSpec((ts, D), lambda b_, i: (i, b_)),
        out_shape=jax.ShapeDtypeStruct((S, B * D), F32),
        compiler_params=_params("parallel", "parallel"),
        name="ln_in",
    )(x, g.reshape(1, D), b.reshape(1, D))


def _qkv_kernel(h_ref, w_ref, bfc_ref, qkv_ref, ccol_ref, kf_ref, carry_c,
                *, n_heads, head_dim, q_scale):
    i = pl.program_id(1)

    @pl.when(i == 0)
    def _():
        carry_c[...] = jnp.zeros_like(carry_c)

    hb = h_ref[...].astype(BF16)
    tm = hb.shape[0]
    d_qkv = 3 * n_heads * head_dim
    fc = jnp.dot(hb, w_ref[:, d_qkv:], preferred_element_type=F32) + bfc_ref[...]
    z = jnp.dot(hb, w_ref[:, :d_qkv], preferred_element_type=F32)
    for j in range(3 * n_heads):
        blk = z[:, j * head_dim:(j + 1) * head_dim]
        if j < n_heads:
            blk = blk * q_scale
        qkv_ref[j] = blk.astype(BF16)

    def log2_forget(f):
        return (jnp.minimum(f, 0.0) - jnp.log1p(jnp.exp(-jnp.abs(f)))) * LOG2E

    row = lax.broadcasted_iota(jnp.int32, (tm, tm), 0)
    col = lax.broadcasted_iota(jnp.int32, (tm, tm), 1)

    lower = jnp.where(col <= row, 1.0, 0.0).astype(BF16)
    cc = jnp.dot(lower, jnp.concatenate(_split3(log2_forget(fc)), axis=1),
                 preferred_element_type=F32)
    cc = cc[:, :LANES] + cc[:, LANES:2 * LANES] + cc[:, 2 * LANES:] + carry_c[...]
    ccol_ref[...] = cc
    carry_c[...] = cc[tm - 1:tm, :]

    lane = lax.broadcasted_iota(jnp.int32, (tm, LANES), 1)
    pieces = [jnp.where(lane < n_heads, piece.astype(F32), 0.0) for piece in _split3(-cc)]
    kf = pieces[0]
    for term in (1, 2):
        kf = kf + pltpu.roll(pieces[term], term * KEY_TERM_STRIDE, 1)
    kf_ref[...] = kf.astype(BF16)


def _qkv_proj(h2, w_qkv, b_fc, l, *, B, S, D, n_heads, head_dim, tm):
    d_att = n_heads * head_dim
    kern = functools.partial(_qkv_kernel, n_heads=n_heads, head_dim=head_dim,
                             q_scale=LOG2E / math.sqrt(head_dim))
    return pl.pallas_call(
        kern,
        grid=(B, S // tm),
        in_specs=[pl.BlockSpec((tm, D), lambda b_, i: (i, b_)),
                  _layer_spec((D, 3 * d_att + LANES), l),
                  _layer_spec((1, LANES), l)],
        out_specs=[pl.BlockSpec((3 * n_heads, None, tm, head_dim), lambda b_, i: (0, b_, i, 0)),
                   pl.BlockSpec((None, tm, LANES), lambda b_, i: (b_, i, 0)),
                   pl.BlockSpec((None, tm, LANES), lambda b_, i: (b_, i, 0))],
        out_shape=[jax.ShapeDtypeStruct((3 * n_heads, B, S, head_dim), BF16),
                   jax.ShapeDtypeStruct((B, S, LANES), F32),
                   jax.ShapeDtypeStruct((B, S, LANES), BF16)],
        scratch_shapes=[pltpu.VMEM((1, LANES), F32)],
        compiler_params=_params("parallel", "arbitrary"),
        name="qkv_proj",
    )(h2, w_qkv, b_fc)


def _attn_kernel(q_ref, k_ref, v_ref, cq_ref, kf_ref, o_ref, *, n_heads, tq, group, wide):
    i = pl.program_id(1)
    n_q_tiles = kf_ref.shape[0] // tq
    lane = lax.broadcasted_iota(jnp.int32, (tq, LANES), 1)
    r_idx = lax.broadcasted_iota(jnp.int32, (tq, tq), 0)
    c_idx = lax.broadcasted_iota(jnp.int32, (tq, tq), 1)
    causal_bias = jnp.where(c_idx <= r_idx, 0.0, -jnp.inf)
    cq_all = cq_ref[...]
    nt = (((1,), (1,)), ((), ()))

    def chain_inputs(h):
        pick = jnp.where(lane < 3 * KEY_TERM_STRIDE,
                         jnp.where((lane & (KEY_TERM_STRIDE - 1)) == h, 1.0, 0.0), 0.0)
        q_aug = jnp.concatenate([q_ref[h], pick.astype(BF16)], axis=1)
        c_q = jnp.sum(jnp.where(lane == h, cq_all, 0.0), axis=1, keepdims=True)
        return q_aug, c_q

    def operands(h, start, width):
        keys = pl.ds(start, width)
        k_aug = jnp.concatenate([k_ref[h, keys, :], kf_ref[keys, :]], axis=1)
        v_aug = jnp.concatenate([v_ref[h, keys, :], jnp.ones((width, LANES), BF16)], axis=1)
        return k_aug, v_aug

    def advance(state, s, v_aug, c_q):
        row_max = c_q + jnp.max(s, axis=1, keepdims=True)
        if state is None:
            p = jnp.exp2(s + (c_q - row_max))
            return row_max, jnp.dot(p.astype(BF16), v_aug, preferred_element_type=F32)
        m, acc = state
        m_new = jnp.maximum(m, row_max)
        alpha = jnp.exp2(m - m_new)
        p = jnp.exp2(s + (c_q - m_new))
        return m_new, alpha * acc + jnp.dot(p.astype(BF16), v_aug, preferred_element_type=F32)

    def head_group(g, _, *, n_full):
        heads = [g * group + u for u in range(group)]
        inputs = [chain_inputs(h) for h in heads]
        state = [None] * group
        spans, j = [], 0
        while j < n_full:
            step = min(wide, n_full - j)
            spans.append((j * tq, step * tq, False))
            j += step
        spans.append((n_full * tq, tq, True))
        for start, width, diagonal in spans:
            for u, (h, (q_aug, c_q)) in enumerate(zip(heads, inputs)):
                k_aug, v_aug = operands(h, start, width)
                s = lax.dot_general(q_aug, k_aug, nt, preferred_element_type=F32)
                if diagonal:
                    s = s + causal_bias
                state[u] = advance(state[u], s, v_aug, c_q)
        for u, h in enumerate(heads):
            acc = state[u][1]
            o_ref[h] = (acc[:, :LANES] / acc[:, LANES:]).astype(o_ref.dtype)
        return 0

    for n_full in range(n_q_tiles):
        @pl.when(i == n_full)
        def _(n_full=n_full):
            lax.fori_loop(0, n_heads // group, functools.partial(head_group, n_full=n_full), 0)


def _attention(qkv, ccol, kf, *, B, S, n_heads, head_dim, tq, group):
    H = n_heads
    kern = functools.partial(_attn_kernel, n_heads=H, tq=tq, group=group, wide=3)
    return pl.pallas_call(
        kern,
        grid=(B, S // tq),
        in_specs=[pl.BlockSpec((H, None, tq, head_dim), lambda b_, i: (0, b_, i, 0)),
                  pl.BlockSpec((H, None, S, head_dim), lambda b_, i: (1, b_, 0, 0)),
                  pl.BlockSpec((H, None, S, head_dim), lambda b_, i: (2, b_, 0, 0)),
                  pl.BlockSpec((None, tq, LANES), lambda b_, i: (b_, i, 0)),
                  pl.BlockSpec((None, S, LANES), lambda b_, i: (b_, 0, 0))],
        out_specs=pl.BlockSpec((H, None, tq, head_dim), lambda b_, i: (0, b_, i, 0)),
        out_shape=jax.ShapeDtypeStruct((H, B, S, head_dim), BF16),
        compiler_params=_params("parallel", "arbitrary"),
        name="fox_attention",
    )(qkv, qkv, qkv, ccol, kf)


def _row_pitch(tt):
    groups = tt // SUBLANES + 1
    return SUBLANES * (groups if groups % 2 else groups + 1)


def _gelu_tanh(x):
    c = math.sqrt(2.0 / math.pi)
    half_x = 0.5 * x
    return half_x + half_x * jnp.tanh(x * (c + (c * 0.044715) * (x * x)))


def _rnn_kernel(h_ref, w_ref, cw_ref, cb_ref, wax_ref, ba_ref, bx_ref, lam_ref,
                o_ref, rx_even, rx_odd, gy_even, gy_odd, x_tb, h_bt, h_carry,
                *, n_batch, n_blocks, blk, pitch, n_tiles):
    i = pl.program_id(0)
    tt = h_ref.shape[0]
    rows, d = tt * n_batch, n_blocks * blk
    halo = (CONV_WIDTH - 1) * n_batch
    chunk = 2 * d // n_blocks

    @pl.when(i == 0)
    def _():
        x_tb[0:halo, :] = jnp.zeros((halo, d), F32)
        h_carry[...] = jnp.zeros_like(h_carry)
        rx_odd[...] = jnp.zeros_like(rx_odd)
        gy_odd[...] = jnp.zeros_like(gy_odd)

    def body(cur, prev, project):
        (rx_cur, gy_cur), (rx_prev, gy_prev) = cur, prev
        half_l2 = (-0.5 * RG_C * LOG2E) * (jnp.maximum(-lam_ref[...], 0.0)
                                           + jnp.log1p(jnp.exp(-jnp.abs(lam_ref[...]))))
        half_ba = 0.5 * ba_ref[...]
        half_bx = 0.5 * bx_ref[...]
        if project:
            hb = jnp.concatenate([h_ref[:, b * d:(b + 1) * d].astype(BF16) for b in range(n_batch)],
                                 axis=0)

        def conv_and_gate_logits(n):
            sl = slice(n * blk, (n + 1) * blk)
            for t in range(tt):
                x_tb[halo + t * n_batch:halo + (t + 1) * n_batch, sl] = (
                    rx_prev[n, pl.ds(t, n_batch, stride=pitch), :])
            half_xc = 0.5 * cb_ref[:, sl] + (0.5 * cw_ref[0:1, sl]) * x_tb[0:rows, sl]
            for k in range(1, CONV_WIDTH):
                half_xc = half_xc + (0.5 * cw_ref[k:k + 1, sl]) * x_tb[k * n_batch:k * n_batch + rows, sl]
            x_tb[0:halo, sl] = x_tb[rows:rows + halo, sl]
            return half_xc, jnp.dot(half_xc.astype(BF16), wax_ref[n], preferred_element_type=F32)

        staged = conv_and_gate_logits(0)
        for n in range(n_blocks):
            sl = slice(n * blk, (n + 1) * blk)
            half_x, g = staged
            if n + 1 < n_blocks:
                staged = conv_and_gate_logits(n + 1)
            t_r = jnp.tanh(g[:, :blk] + half_ba[:, sl])
            t_i = jnp.tanh(g[:, blk:] + half_bx[:, sl])
            a = jnp.exp2(half_l2[:, sl] + half_l2[:, sl] * t_r)
            one_m_a2 = 1.0 - a * a
            mult = one_m_a2 * lax.rsqrt(jnp.maximum(one_m_a2, SQRT_ARG_FLOOR))
            u = mult * (half_x + half_x * t_i)

            h = h_carry[:, sl]
            for t in range(tt):
                h = a[t * n_batch:(t + 1) * n_batch, :] * h + u[t * n_batch:(t + 1) * n_batch, :]
                h_bt[n, pl.ds(t, n_batch, stride=pitch), :] = h
            h_carry[:, sl] = jnp.where(i > 0, h, 0.0)

            for b in range(n_batch):
                o_ref[:, b * d + n * blk:b * d + (n + 1) * blk] = (
                    h_bt[n, b * pitch:b * pitch + tt, :] * gy_prev[b * tt:(b + 1) * tt, sl]
                ).astype(o_ref.dtype)

            if not project:
                continue
            zc = jnp.dot(hb, w_ref[:, n * chunk:(n + 1) * chunk], preferred_element_type=F32)
            for sub in range(chunk // blk):
                col = n * chunk + sub * blk
                piece = zc[:, sub * blk:(sub + 1) * blk]
                if col < d:
                    for b in range(n_batch):
                        rx_cur[col // blk, b * pitch:b * pitch + tt, :] = piece[b * tt:(b + 1) * tt, :]
                else:
                    gy_cur[:, col - d:col - d + blk] = _gelu_tanh(piece)

    _by_parity(i, n_tiles, body, (rx_even, gy_even), (rx_odd, gy_odd))


def _rnn_branch(h2, w_r, conv_w, conv_b, w_ax, b_a, b_x, lam, l, *, B, S, D, n_blocks, tt):
    rows = tt * B
    blk = D // n_blocks
    pitch = _row_pitch(tt)
    n_tiles = S // tt
    kern = functools.partial(_rnn_kernel, n_batch=B, n_blocks=n_blocks, blk=blk, pitch=pitch,
                             n_tiles=n_tiles)
    slabs = pltpu.VMEM((n_blocks, B * pitch, blk), F32)
    return pl.pallas_call(
        kern,
        grid=(n_tiles + 1,),
        in_specs=[pl.BlockSpec((tt, B * D), lambda i: (jnp.minimum(i, n_tiles - 1), 0)),
                  _layer_spec((D, 2 * D), l),
                  _layer_spec((CONV_WIDTH, D), l),
                  _layer_spec((1, D), l),
                  _layer_spec((n_blocks, blk, 2 * blk), l),
                  _layer_spec((1, D), l), _layer_spec((1, D), l), _layer_spec((1, D), l)],
        out_specs=pl.BlockSpec((tt, B * D), lambda i: (jnp.maximum(i - 1, 0), 0)),
        out_shape=jax.ShapeDtypeStruct((S, B * D), BF16),
        scratch_shapes=[slabs, slabs,
                        pltpu.VMEM((rows, D), F32), pltpu.VMEM((rows, D), F32),
                        pltpu.VMEM((rows + (CONV_WIDTH - 1) * B, D), F32),
                        slabs,
                        pltpu.VMEM((B, D), F32)],
        compiler_params=_params("arbitrary"),
        name="rglru_branch",
    )(h2, w_r, conv_w, conv_b, w_ax, b_a, b_x, lam)


def _by_parity(step, n_tiles, body, even, odd):
    parity = lax.rem(step, 2)
    roles = ((even, odd), (odd, even))
    for par, (cur, prev) in enumerate(roles):
        if par == n_tiles % 2:
            pl.when(jnp.logical_and(parity == par, step < n_tiles))(
                functools.partial(body, cur, prev, True))
            pl.when(step == n_tiles)(functools.partial(body, cur, prev, False))
        else:
            pl.when(parity == par)(functools.partial(body, cur, prev, True))


def _tile_maps(n_tiles, tiles_per_batch):
    def head(g):
        t = jnp.minimum(g, n_tiles - 1)
        return t % tiles_per_batch, t // tiles_per_batch

    def tail(g):
        t = jnp.maximum(g - 1, 0)
        return t % tiles_per_batch, t // tiles_per_batch
    return head, tail


def _merge_kernel(h_ref, att_ref, rnn_ref, wg_ref, wa_ref, wr_ref, wo_ref, bm_ref, g_ref, b_ref,
                  o_ref, y_even, y_odd, *, n_heads, alpha, n_tiles):
    step = pl.program_id(0)

    @pl.when(step == 0)
    def _():
        y_odd[...] = jnp.zeros_like(y_odd)

    def body(y_cur, y_prev, project):
        if project:
            h = h_ref[...]
            d = h.shape[1]
            gates = jnp.dot(h.astype(BF16), wg_ref[...], preferred_element_type=F32)
        o_ref[...] = _layer_norm(y_prev[...], g_ref[...], b_ref[...])
        if project:
            att = jnp.concatenate([att_ref[j] for j in range(n_heads)], axis=1)
            ya = jnp.dot(att, wa_ref[...], preferred_element_type=F32)
            yb = jnp.dot(rnn_ref[...], wr_ref[...], preferred_element_type=F32)
            t_a = jnp.tanh(0.5 * (gates[:, :d] + bm_ref[0:1, :]))
            t_b = jnp.tanh(0.5 * (gates[:, d:] + bm_ref[1:2, :]))
            half_ya, half_yb = 0.5 * ya, 0.5 * yb
            merged = (half_ya + half_ya * t_a) + (half_yb + half_yb * t_b)
            m = jnp.dot(merged.astype(BF16), wo_ref[...], preferred_element_type=F32)
            y_cur[...] = alpha * h + m

    _by_parity(step, n_tiles, body, y_even, y_odd)


def _merge(h2, att, rnn2, w_g, w_a, w_r, w_o, b_m, ln_g, ln_b, l, *, B, S, D, n_heads, head_dim,
           ts, alpha):
    kern = functools.partial(_merge_kernel, n_heads=n_heads, alpha=alpha, n_tiles=B * (S // ts))
    d_att = n_heads * head_dim
    n_tiles = B * (S // ts)
    head, tail = _tile_maps(n_tiles, S // ts)
    tok = pl.BlockSpec((ts, D), lambda g: head(g))
    return pl.pallas_call(
        kern,
        grid=(n_tiles + 1,),
        in_specs=[tok,
                  pl.BlockSpec((n_heads, None, ts, head_dim),
                               lambda g: (0, head(g)[1], head(g)[0], 0)),
                  tok,
                  _layer_spec((D, 2 * D), l), _layer_spec((d_att, D), l),
                  _layer_spec((D, D), l), _layer_spec((D, D), l),
                  _layer_spec((2, D), l), _layer_spec((1, D), l), _layer_spec((1, D), l)],
        out_specs=pl.BlockSpec((ts, D), lambda g: tail(g)),
        out_shape=jax.ShapeDtypeStruct((S, B * D), F32),
        scratch_shapes=[pltpu.VMEM((ts, D), F32), pltpu.VMEM((ts, D), F32)],
        compiler_params=_params("arbitrary"),
        name="merge_out",
    )(h2, att, rnn2, w_g, w_a, w_r, w_o, b_m, ln_g, ln_b)


def _ffn_kernel(h_ref, p_ref, wi_ref, wo_ref, g1_ref, b1_ref, wpg_ref, bpg_ref, wp_ref,
                g2_ref, b2_ref, o_ref, y_even, y_odd, *, d_ff, chunks, alpha, n_tiles):
    step = pl.program_id(0)

    @pl.when(step == 0)
    def _():
        y_odd[...] = jnp.zeros_like(y_odd)

    def body(y_cur, y_prev, project):
        def previous_tile_inputs():
            h1 = _layer_norm(y_prev[...], g1_ref[...], b1_ref[...])
            return (h1, jnp.dot(h1.astype(BF16), wpg_ref[...], preferred_element_type=F32),
                    jnp.dot(p_ref[...].astype(BF16), wp_ref[...], preferred_element_type=F32))

        if not project:
            finish_previous_tile(*previous_tile_inputs())
            return
        h = h_ref[...]
        hb = h.astype(BF16)

        def up(lo, hi):
            return (jnp.dot(hb, wi_ref[:, lo:hi], preferred_element_type=F32),
                    jnp.dot(hb, wi_ref[:, d_ff + lo:d_ff + hi], preferred_element_type=F32))

        def down(half_g, hu, lo, hi):
            act = ((half_g + half_g * jnp.tanh(half_g)) * hu).astype(BF16)
            return jnp.dot(act, wo_ref[lo:hi, :], preferred_element_type=F32)

        f = down(*up(*chunks[0]), *chunks[0])
        tail_inputs = previous_tile_inputs()
        for lo, hi in chunks[1:2]:
            f = f + down(*up(lo, hi), lo, hi)
        finish_previous_tile(*tail_inputs)
        for lo, hi in chunks[2:]:
            f = f + down(*up(lo, hi), lo, hi)
        y_cur[...] = alpha * h + f

    def finish_previous_tile(h1, gate_logits, emb):
        gated = emb + emb * jnp.tanh(gate_logits + 0.5 * bpg_ref[...])
        o_ref[...] = _layer_norm(alpha * h1 + gated, g2_ref[...], b2_ref[...])

    _by_parity(step, n_tiles, body, y_even, y_odd)


def _ffn_chunks(d_ff, width):
    edges = list(range(0, d_ff, width)) + [d_ff]
    return tuple(zip(edges[:-1], edges[1:]))


def _ffn_ple(h2, p, w_i, w_o, g1, b1, w_pg, b_pg, w_p, g2, b2, l, *, B, S, D, d_ff, d_ple, ts,
             alpha, batch_major_out):
    n_tiles = B * (S // ts)
    kern = functools.partial(_ffn_kernel, d_ff=d_ff, chunks=_ffn_chunks(d_ff, FFN_CHUNK_COLS),
                             alpha=alpha, n_tiles=n_tiles)
    head, tail = _tile_maps(n_tiles, S // ts)
    if batch_major_out:
        out_spec = pl.BlockSpec((None, ts, D), lambda g: (tail(g)[1], tail(g)[0], 0))
        out_shape = jax.ShapeDtypeStruct((B, S, D), F32)
    else:
        out_spec = pl.BlockSpec((ts, D), lambda g: tail(g))
        out_shape = jax.ShapeDtypeStruct((S, B * D), F32)
    vec = _layer_spec((1, D), l)
    return pl.pallas_call(
        kern,
        grid=(n_tiles + 1,),
        in_specs=[pl.BlockSpec((ts, D), lambda g: head(g)),
                  pl.BlockSpec((None, None, ts, d_ple), lambda g: (l, tail(g)[1], tail(g)[0], 0)),
                  _layer_spec((D, 2 * d_ff), l), _layer_spec((d_ff, D), l), vec, vec,
                  _layer_spec((D, D), l), vec, _layer_spec((d_ple, D), l), vec, vec],
        out_specs=out_spec,
        out_shape=out_shape,
        scratch_shapes=[pltpu.VMEM((ts, D), F32), pltpu.VMEM((ts, D), F32)],
        compiler_params=_params("arbitrary"),
        name="ffn_ple",
    )(h2, p, w_i, w_o, g1, b1, w_pg, b_pg, w_p, g2, b2)


def kernel(x, p, ln_in_g, ln_in_b, w_in, b_forget, conv_w, conv_b, rg_w_a, rg_b_a, rg_w_x, rg_b_x,
           rg_lambda, w_branch_att, w_branch_rnn, b_merge, w_out, ln_mix_g, ln_mix_b, w_ffn_in,
           w_ffn_out, ln_ffn_g, ln_ffn_b, w_ple, w_ple_gate, b_ple_gate, ln_ple_g, ln_ple_b):
    B, S, D = x.shape
    L = w_in.shape[0]
    H = b_forget.shape[1]
    d_att = w_branch_att.shape[1]
    head_dim = d_att // H
    n_blocks, blk = rg_w_a.shape[1], rg_w_a.shape[2]
    d_ff = w_ffn_out.shape[1]
    d_ple = w_ple.shape[1]
    assert head_dim == LANES and blk == LANES and D == n_blocks * blk and d_att == D
    assert H <= KEY_TERM_STRIDE and B % SUBLANES == 0
    alpha = float((2 * L) ** 0.25)

    ts = min(ROW_TILE, S)
    tq = min(ROW_TILE, S)
    tt = max(1, min(S, ROW_TILE // B))

    o_f, o_rx = 3 * d_att, 3 * d_att + H
    o_g = o_rx + 2 * D
    w_qkv = jnp.pad(w_in[:, :, :o_rx], ((0, 0), (0, 0), (0, LANES - H))).astype(BF16)
    b_fc = jnp.pad(b_forget, ((0, 0), (0, LANES - H))).reshape(L, 1, LANES)
    w_r = w_in[:, :, o_rx:o_g].astype(BF16)
    w_g = w_in[:, :, o_g:].astype(BF16)
    w_ax = jnp.concatenate([rg_w_a, rg_w_x], axis=-1).astype(BF16)
    w_ba, w_br, w_o = (w.astype(BF16) for w in (w_branch_att, w_branch_rnn, w_out))
    gate_half = jnp.where(jnp.arange(2 * d_ff) < d_ff, 0.5, 1.0).astype(F32)
    w_fi, w_fo = (w_ffn_in * gate_half).astype(BF16), w_ffn_out.astype(BF16)
    w_pg, w_p = (0.5 * w_ple_gate).astype(BF16), (0.5 * w_ple).astype(BF16)
    vec = lambda a: a.reshape(L, 1, D)

    h2 = _ln_in(x, ln_in_g, ln_in_b, min(2 * ROW_TILE, S))
    for l in range(L):
        qkv, ccol, kf = _qkv_proj(h2, w_qkv, b_fc, l, B=B, S=S, D=D,
                                  n_heads=H, head_dim=head_dim, tm=ts)
        att = _attention(qkv, ccol, kf, B=B, S=S, n_heads=H, head_dim=head_dim, tq=tq, group=H)
        rnn = _rnn_branch(h2, w_r, conv_w, vec(conv_b), w_ax, vec(rg_b_a),
                          vec(rg_b_x), vec(rg_lambda), l, B=B, S=S, D=D, n_blocks=n_blocks, tt=tt)
        h2 = _merge(h2, att, rnn, w_g, w_ba, w_br, w_o, b_merge,
                    vec(ln_mix_g), vec(ln_mix_b), l, B=B, S=S, D=D, n_heads=H, head_dim=head_dim,
                    ts=ts, alpha=alpha)
        h2 = _ffn_ple(h2, p, w_fi, w_fo, vec(ln_ffn_g), vec(ln_ffn_b), w_pg, vec(b_ple_gate), w_p,
                      vec(ln_ple_g), vec(ln_ple_b), l, B=B, S=S, D=D, d_ff=d_ff, d_ple=d_ple,
                      ts=ts, alpha=alpha, batch_major_out=(l == L - 1))
    return h2
```

```python
import functools
import math

import jax
import jax.numpy as jnp
from jax import lax
from jax.experimental import pallas as pl
from jax.experimental.pallas import tpu as pltpu

LN_EPS = 1e-5
RG_C = 8.0
CONV_WIDTH = 4
LOG2E = 1.4426950408889634
SQRT_ARG_FLOOR = 1e-30
LANES = 128
SUBLANES = 8
KEY_TERM_SHIFT = 3
KEY_TERM_STRIDE = 1 << KEY_TERM_SHIFT
VMEM_LIMIT_BYTES = 56 * 1024 * 1024
ROW_TILE = 512
FFN_CHUNK_COLS = 1024

F32 = jnp.float32
BF16 = jnp.bfloat16


def _params(*semantics):
    return pltpu.CompilerParams(dimension_semantics=semantics,
                                vmem_limit_bytes=VMEM_LIMIT_BYTES)


def _layer_spec(tail, l):
    zeros = (0,) * len(tail)
    return pl.BlockSpec((None,) + tuple(tail), lambda *_: (l,) + zeros,
                        pipeline_mode=pl.Buffered(1))


def _const_spec(shape):
    zeros = (0,) * len(shape)
    return pl.BlockSpec(tuple(shape), lambda *_: zeros, pipeline_mode=pl.Buffered(1))


def _layer_norm(x, g, b):
    mu = jnp.mean(x, axis=-1, keepdims=True)
    xc = x - mu
    var = jnp.mean(xc * xc, axis=-1, keepdims=True)
    return xc * lax.rsqrt(var + LN_EPS) * g + b


def _split3(x):
    hi = x.astype(BF16)
    r1 = x - hi.astype(F32)
    mid = r1.astype(BF16)
    lo = (r1 - mid.astype(F32)).astype(BF16)
    return hi, mid, lo


def _ln_in_kernel(x_ref, g_ref, b_ref, o_ref, o16_ref):
    y = _layer_norm(x_ref[...], g_ref[...], b_ref[...])
    o_ref[...] = y
    o16_ref[...] = y.astype(BF16)


def _ln_in(x, g, b, ts):
    B, S, D = x.shape
    tok = pl.BlockSpec((ts, D), lambda b_, i: (i, b_))
    return pl.pallas_call(
        _ln_in_kernel,
        grid=(B, S // ts),
        in_specs=[pl.BlockSpec((None, ts, D), lambda b_, i: (b_, i, 0)),
                  _const_spec((1, D)), _const_spec((1, D))],
        out_specs=[tok, tok],
        out_shape=[jax.ShapeDtypeStruct((S, B * D), F32), jax.ShapeDtypeStruct((S, B * D), BF16)],
        compiler_params=_params("parallel", "parallel"),
        name="ln_in",
    )(x, g.reshape(1, D), b.reshape(1, D))


def _qkv_kernel(h_ref, w_ref, bfc_ref, qkv_ref, ccol_ref, kf_ref, carry_c,
                *, n_heads, head_dim, q_scale):
    i = pl.program_id(1)

    @pl.when(i == 0)
    def _():
        carry_c[...] = jnp.zeros_like(carry_c)

    hb = h_ref[...].astype(BF16)
    tm = hb.shape[0]
    d_qkv = 3 * n_heads * head_dim
    fc = jnp.dot(hb, w_ref[:, d_qkv:], preferred_element_type=F32) + bfc_ref[...]
    z = jnp.dot(hb, w_ref[:, :d_qkv], preferred_element_type=F32)
    for j in range(3 * n_heads):
        blk = z[:, j * head_dim:(j + 1) * head_dim]
        if j < n_heads:
            blk = blk * q_scale
        qkv_ref[j] = blk.astype(BF16)

    def log2_forget(f):
        return (jnp.minimum(f, 0.0) - jnp.log1p(jnp.exp(-jnp.abs(f)))) * LOG2E

    row = lax.broadcasted_iota(jnp.int32, (tm, tm), 0)
    col = lax.broadcasted_iota(jnp.int32, (tm, tm), 1)

    lower = jnp.where(col <= row, 1.0, 0.0).astype(BF16)
    cc = jnp.dot(lower, jnp.concatenate(_split3(log2_forget(fc)), axis=1),
                 preferred_element_type=F32)
    cc = cc[:, :LANES] + cc[:, LANES:2 * LANES] + cc[:, 2 * LANES:] + carry_c[...]
    ccol_ref[...] = cc
    carry_c[...] = cc[tm - 1:tm, :]

    lane = lax.broadcasted_iota(jnp.int32, (tm, LANES), 1)
    pieces = [jnp.where(lane < n_heads, piece.astype(F32), 0.0) for piece in _split3(-cc)]
    kf = pieces[0]
    for term in (1, 2):
        kf = kf + pltpu.roll(pieces[term], term * KEY_TERM_STRIDE, 1)
    kf_ref[...] = kf.astype(BF16)


def _qkv_proj(h2, w_qkv, b_fc, l, *, B, S, D, n_heads, head_dim, tm):
    d_att = n_heads * head_dim
    kern = functools.partial(_qkv_kernel, n_heads=n_heads, head_dim=head_dim,
                             q_scale=LOG2E / math.sqrt(head_dim))
    return pl.pallas_call(
        kern,
        grid=(B, S // tm),
        in_specs=[pl.BlockSpec((tm, D), lambda b_, i: (i, b_)),
                  _layer_spec((D, 3 * d_att + LANES), l),
                  _layer_spec((1, LANES), l)],
        out_specs=[pl.BlockSpec((3 * n_heads, None, tm, head_dim), lambda b_, i: (0, b_, i, 0)),
                   pl.BlockSpec((None, tm, LANES), lambda b_, i: (b_, i, 0)),
                   pl.BlockSpec((None, tm, LANES), lambda b_, i: (b_, i, 0))],
        out_shape=[jax.ShapeDtypeStruct((3 * n_heads, B, S, head_dim), BF16),
                   jax.ShapeDtypeStruct((B, S, LANES), F32),
                   jax.ShapeDtypeStruct((B, S, LANES), BF16)],
        scratch_shapes=[pltpu.VMEM((1, LANES), F32)],
        compiler_params=_params("parallel", "arbitrary"),
        name="qkv_proj",
    )(h2, w_qkv, b_fc)


def _attn_kernel(q_ref, k_ref, v_ref, cq_ref, kf_ref, o_ref, *, n_heads, tq, group, wide):
    i = pl.program_id(1)
    n_q_tiles = kf_ref.shape[0] // tq
    lane = lax.broadcasted_iota(jnp.int32, (tq, LANES), 1)
    r_idx = lax.broadcasted_iota(jnp.int32, (tq, tq), 0)
    c_idx = lax.broadcasted_iota(jnp.int32, (tq, tq), 1)
    causal_bias = jnp.where(c_idx <= r_idx, 0.0, -jnp.inf)
    cq_all = cq_ref[...]
    nt = (((1,), (1,)), ((), ()))

    def chain_inputs(h):
        pick = jnp.where(lane < 3 * KEY_TERM_STRIDE,
                         jnp.where((lane & (KEY_TERM_STRIDE - 1)) == h, 1.0, 0.0), 0.0)
        q_aug = jnp.concatenate([q_ref[h], pick.astype(BF16)], axis=1)
        c_q = jnp.sum(jnp.where(lane == h, cq_all, 0.0), axis=1, keepdims=True)
        return q_aug, c_q

    def operands(h, start, width):
        keys = pl.ds(start, width)
        k_aug = jnp.concatenate([k_ref[h, keys, :], kf_ref[keys, :]], axis=1)
        v_aug = jnp.concatenate([v_ref[h, keys, :], jnp.ones((width, LANES), BF16)], axis=1)
        return k_aug, v_aug

    def advance(state, s, v_aug, c_q):
        row_max = c_q + jnp.max(s, axis=1, keepdims=True)
        if state is None:
            p = jnp.exp2(s + (c_q - row_max))
            return row_max, jnp.dot(p.astype(BF16), v_aug, preferred_element_type=F32)
        m, acc = state
        m_new = jnp.maximum(m, row_max)
        alpha = jnp.exp2(m - m_new)
        p = jnp.exp2(s + (c_q - m_new))
        return m_new, alpha * acc + jnp.dot(p.astype(BF16), v_aug, preferred_element_type=F32)

    def head_group(g, _, *, n_full):
        heads = [g * group + u for u in range(group)]
        inputs = [chain_inputs(h) for h in heads]
        state = [None] * group
        spans, j = [], 0
        while j < n_full:
            step = min(wide, n_full - j)
            spans.append((j * tq, step * tq, False))
            j += step
        spans.append((n_full * tq, tq, True))
        for start, width, diagonal in spans:
            for u, (h, (q_aug, c_q)) in enumerate(zip(heads, inputs)):
                k_aug, v_aug = operands(h, start, width)
                s = lax.dot_general(q_aug, k_aug, nt, preferred_element_type=F32)
                if diagonal:
                    s = s + causal_bias
                state[u] = advance(state[u], s, v_aug, c_q)
        for u, h in enumerate(heads):
            acc = state[u][1]
            o_ref[h] = (acc[:, :LANES] / acc[:, LANES:]).astype(o_ref.dtype)
        return 0

    for n_full in range(n_q_tiles):
        @pl.when(i == n_full)
        def _(n_full=n_full):
            lax.fori_loop(0, n_heads // group, functools.partial(head_group, n_full=n_full), 0)


def _attention(qkv, ccol, kf, *, B, S, n_heads, head_dim, tq, group):
    H = n_heads
    kern = functools.partial(_attn_kernel, n_heads=H, tq=tq, group=group, wide=3)
    return pl.pallas_call(
        kern,
        grid=(B, S // tq),
        in_specs=[pl.BlockSpec((H, None, tq, head_dim), lambda b_, i: (0, b_, i, 0)),
                  pl.BlockSpec((H, None, S, head_dim), lambda b_, i: (1, b_, 0, 0)),
                  pl.BlockSpec((H, None, S, head_dim), lambda b_, i: (2, b_, 0, 0)),
                  pl.BlockSpec((None, tq, LANES), lambda b_, i: (b_, i, 0)),
                  pl.BlockSpec((None, S, LANES), lambda b_, i: (b_, 0, 0))],
        out_specs=pl.BlockSpec((H, None, tq, head_dim), lambda b_, i: (0, b_, i, 0)),
        out_shape=jax.ShapeDtypeStruct((H, B, S, head_dim), BF16),
        compiler_params=_params("parallel", "arbitrary"),
        name="fox_attention",
    )(qkv, qkv, qkv, ccol, kf)


def _row_pitch(tt):
    groups = tt // SUBLANES + 1
    return SUBLANES * (groups if groups % 2 else groups + 1)


def _gelu_tanh(x):
    c = math.sqrt(2.0 / math.pi)
    half_x = 0.5 * x
    return half_x + half_x * jnp.tanh(x * (c + (c * 0.044715) * (x * x)))


def _rnn_kernel(h_ref, w_ref, cw_ref, cb_ref, wax_ref, ba_ref, bx_ref, lam_ref,
                o_ref, rx_even, rx_odd, gy_even, gy_odd, x_tb, h_bt, h_carry,
                *, n_batch, n_blocks, blk, pitch, n_tiles):
    i = pl.program_id(0)
    tt = h_ref.shape[0]
    rows, d = tt * n_batch, n_blocks * blk
    halo = (CONV_WIDTH - 1) * n_batch
    chunk = 2 * d // n_blocks

    @pl.when(i == 0)
    def _():
        x_tb[0:halo, :] = jnp.zeros((halo, d), F32)
        h_carry[...] = jnp.zeros_like(h_carry)
        rx_odd[...] = jnp.zeros_like(rx_odd)
        gy_odd[...] = jnp.zeros_like(gy_odd)

    def body(cur, prev, project):
        (rx_cur, gy_cur), (rx_prev, gy_prev) = cur, prev
        half_l2 = (-0.5 * RG_C * LOG2E) * (jnp.maximum(-lam_ref[...], 0.0)
                                           + jnp.log1p(jnp.exp(-jnp.abs(lam_ref[...]))))
        half_ba = 0.5 * ba_ref[...]
        half_bx = 0.5 * bx_ref[...]
        if project:
            hb = jnp.concatenate([h_ref[:, b * d:(b + 1) * d].astype(BF16) for b in range(n_batch)],
                                 axis=0)

        def conv_and_gate_logits(n):
            sl = slice(n * blk, (n + 1) * blk)
            for t in range(tt):
                x_tb[halo + t * n_batch:halo + (t + 1) * n_batch, sl] = (
                    rx_prev[n, pl.ds(t, n_batch, stride=pitch), :])
            half_xc = 0.5 * cb_ref[:, sl] + (0.5 * cw_ref[0:1, sl]) * x_tb[0:rows, sl]
            for k in range(1, CONV_WIDTH):
                half_xc = half_xc + (0.5 * cw_ref[k:k + 1, sl]) * x_tb[k * n_batch:k * n_batch + rows, sl]
            x_tb[0:halo, sl] = x_tb[rows:rows + halo, sl]
            return half_xc, jnp.dot(half_xc.astype(BF16), wax_ref[n], preferred_element_type=F32)

        staged = conv_and_gate_logits(0)
        for n in range(n_blocks):
            sl = slice(n * blk, (n + 1) * blk)
            half_x, g = staged
            if n + 1 < n_blocks:
                staged = conv_and_gate_logits(n + 1)
            t_r = jnp.tanh(g[:, :blk] + half_ba[:, sl])
            t_i = jnp.tanh(g[:, blk:] + half_bx[:, sl])
            a = jnp.exp2(half_l2[:, sl] + half_l2[:, sl] * t_r)
            one_m_a2 = 1.0 - a * a
            mult = one_m_a2 * lax.rsqrt(jnp.maximum(one_m_a2, SQRT_ARG_FLOOR))
            u = mult * (half_x + half_x * t_i)

            h = h_carry[:, sl]
            for t in range(tt):
                h = a[t * n_batch:(t + 1) * n_batch, :] * h + u[t * n_batch:(t + 1) * n_batch, :]
                h_bt[n, pl.ds(t, n_batch, stride=pitch), :] = h
            h_carry[:, sl] = jnp.where(i > 0, h, 0.0)

            for b in range(n_batch):
                o_ref[:, b * d + n * blk:b * d + (n + 1) * blk] = (
                    h_bt[n, b * pitch:b * pitch + tt, :] * gy_prev[b * tt:(b + 1) * tt, sl]
                ).astype(o_ref.dtype)

            if not project:
                continue
            zc = jnp.dot(hb, w_ref[:, n * chunk:(n + 1) * chunk], preferred_element_type=F32)
            for sub in range(chunk // blk):
                col = n * chunk + sub * blk
                piece = zc[:, sub * blk:(sub + 1) * blk]
                if col < d:
                    for b in range(n_batch):
                        rx_cur[col // blk, b * pitch:b * pitch + tt, :] = piece[b * tt:(b + 1) * tt, :]
                else:
                    gy_cur[:, col - d:col - d + blk] = _gelu_tanh(piece)

    _by_parity(i, n_tiles, body, (rx_even, gy_even), (rx_odd, gy_odd))


def _rnn_branch(h2, w_r, conv_w, conv_b, w_ax, b_a, b_x, lam, l, *, B, S, D, n_blocks, tt):
    rows = tt * B
    blk = D // n_blocks
    pitch = _row_pitch(tt)
    n_tiles = S // tt
    kern = functools.partial(_rnn_kernel, n_batch=B, n_blocks=n_blocks, blk=blk, pitch=pitch,
                             n_tiles=n_tiles)
    slabs = pltpu.VMEM((n_blocks, B * pitch, blk), F32)
    return pl.pallas_call(
        kern,
        grid=(n_tiles + 1,),
        in_specs=[pl.BlockSpec((tt, B * D), lambda i: (jnp.minimum(i, n_tiles - 1), 0)),
                  _layer_spec((D, 2 * D), l),
                  _layer_spec((CONV_WIDTH, D), l),
                  _layer_spec((1, D), l),
                  _layer_spec((n_blocks, blk, 2 * blk), l),
                  _layer_spec((1, D), l), _layer_spec((1, D), l), _layer_spec((1, D), l)],
        out_specs=pl.BlockSpec((tt, B * D), lambda i: (jnp.maximum(i - 1, 0), 0)),
        out_shape=jax.ShapeDtypeStruct((S, B * D), BF16),
        scratch_shapes=[slabs, slabs,
                        pltpu.VMEM((rows, D), F32), pltpu.VMEM((rows, D), F32),
                        pltpu.VMEM((rows + (CONV_WIDTH - 1) * B, D), F32),
                        slabs,
                        pltpu.VMEM((B, D), F32)],
        compiler_params=_params("arbitrary"),
        name="rglru_branch",
    )(h2, w_r, conv_w, conv_b, w_ax, b_a, b_x, lam)


def _by_parity(step, n_tiles, body, even, odd):
    parity = lax.rem(step, 2)
    roles = ((even, odd), (odd, even))
    for par, (cur, prev) in enumerate(roles):
        if par == n_tiles % 2:
            pl.when(jnp.logical_and(parity == par, step < n_tiles))(
                functools.partial(body, cur, prev, True))
            pl.when(step == n_tiles)(functools.partial(body, cur, prev, False))
        else:
            pl.when(parity == par)(functools.partial(body, cur, prev, True))


def _tile_maps(n_tiles, tiles_per_batch):
    def head(g):
        t = jnp.minimum(g, n_tiles - 1)
        return t % tiles_per_batch, t // tiles_per_batch

    def tail(g):
        t = jnp.maximum(g - 1, 0)
        return t % tiles_per_batch, t // tiles_per_batch
    return head, tail


def _merge_kernel(h_ref, att_ref, rnn_ref, wg_ref, wa_ref, wr_ref, wo_ref, bm_ref, g_ref, b_ref,
                  o_ref, y_even, y_odd, *, n_heads, alpha, n_tiles):
    step = pl.program_id(0)

    @pl.when(step == 0)
    def _():
        y_odd[...] = jnp.zeros_like(y_odd)

    def body(y_cur, y_prev, project):
        if project:
            h = h_ref[...]
            d = h.shape[1]
            gates = jnp.dot(h.astype(BF16), wg_ref[...], preferred_element_type=F32)
        o_ref[...] = _layer_norm(y_prev[...], g_ref[...], b_ref[...])
        if project:
            att = jnp.concatenate([att_ref[j] for j in range(n_heads)], axis=1)
            ya = jnp.dot(att, wa_ref[...], preferred_element_type=F32)
            yb = jnp.dot(rnn_ref[...], wr_ref[...], preferred_element_type=F32)
            t_a = jnp.tanh(0.5 * (gates[:, :d] + bm_ref[0:1, :]))
            t_b = jnp.tanh(0.5 * (gates[:, d:] + bm_ref[1:2, :]))
            half_ya, half_yb = 0.5 * ya, 0.5 * yb
            merged = (half_ya + half_ya * t_a) + (half_yb + half_yb * t_b)
            m = jnp.dot(merged.astype(BF16), wo_ref[...], preferred_element_type=F32)
            y_cur[...] = alpha * h + m

    _by_parity(step, n_tiles, body, y_even, y_odd)


def _merge(h2, att, rnn2, w_g, w_a, w_r, w_o, b_m, ln_g, ln_b, l, *, B, S, D, n_heads, head_dim,
           ts, alpha):
    kern = functools.partial(_merge_kernel, n_heads=n_heads, alpha=alpha, n_tiles=B * (S // ts))
    d_att = n_heads * head_dim
    n_tiles = B * (S // ts)
    head, tail = _tile_maps(n_tiles, S // ts)
    tok = pl.BlockSpec((ts, D), lambda g: head(g))
    return pl.pallas_call(
        kern,
        grid=(n_tiles + 1,),
        in_specs=[tok,
                  pl.BlockSpec((n_heads, None, ts, head_dim),
                               lambda g: (0, head(g)[1], head(g)[0], 0)),
                  tok,
                  _layer_spec((D, 2 * D), l), _layer_spec((d_att, D), l),
                  _layer_spec((D, D), l), _layer_spec((D, D), l),
                  _layer_spec((2, D), l), _layer_spec((1, D), l), _layer_spec((1, D), l)],
        out_specs=pl.BlockSpec((ts, D), lambda g: tail(g)),
        out_shape=jax.ShapeDtypeStruct((S, B * D), F32),
        scratch_shapes=[pltpu.VMEM((ts, D), F32), pltpu.VMEM((ts, D), F32)],
        compiler_params=_params("arbitrary"),
        name="merge_out",
    )(h2, att, rnn2, w_g, w_a, w_r, w_o, b_m, ln_g, ln_b)


def _ffn_kernel(h_ref, p_ref, wi_ref, wo_ref, g1_ref, b1_ref, wpg_ref, bpg_ref, wp_ref,
                g2_ref, b2_ref, o_ref, *rest, d_ff, chunks, alpha, n_tiles):
    step = pl.program_id(0)
    *maybe_o16, y_even, y_odd = rest
    o16_ref = maybe_o16[0] if maybe_o16 else None

    @pl.when(step == 0)
    def _():
        y_odd[...] = jnp.zeros_like(y_odd)

    def body(y_cur, y_prev, project):
        def previous_tile_inputs():
            h1 = _layer_norm(y_prev[...], g1_ref[...], b1_ref[...])
            return (h1, jnp.dot(h1.astype(BF16), wpg_ref[...], preferred_element_type=F32),
                    jnp.dot(p_ref[...].astype(BF16), wp_ref[...], preferred_element_type=F32))

        if not project:
            finish_previous_tile(*previous_tile_inputs())
            return
        h = h_ref[...]
        hb = h.astype(BF16)

        def up(lo, hi):
            return (jnp.dot(hb, wi_ref[:, lo:hi], preferred_element_type=F32),
                    jnp.dot(hb, wi_ref[:, d_ff + lo:d_ff + hi], preferred_element_type=F32))

        def down(half_g, hu, lo, hi):
            act = ((half_g + half_g * jnp.tanh(half_g)) * hu).astype(BF16)
            return jnp.dot(act, wo_ref[lo:hi, :], preferred_element_type=F32)

        f = down(*up(*chunks[0]), *chunks[0])
        tail_inputs = previous_tile_inputs()
        for lo, hi in chunks[1:2]:
            f = f + down(*up(lo, hi), lo, hi)
        finish_previous_tile(*tail_inputs)
        for lo, hi in chunks[2:]:
            f = f + down(*up(lo, hi), lo, hi)
        y_cur[...] = alpha * h + f

    def finish_previous_tile(h1, gate_logits, emb):
        gated = emb + emb * jnp.tanh(gate_logits + 0.5 * bpg_ref[...])
        y = _layer_norm(alpha * h1 + gated, g2_ref[...], b2_ref[...])
        o_ref[...] = y
        if o16_ref is not None:
            o16_ref[...] = y.astype(BF16)

    _by_parity(step, n_tiles, body, y_even, y_odd)


def _ffn_chunks(d_ff, width):
    edges = list(range(0, d_ff, width)) + [d_ff]
    return tuple(zip(edges[:-1], edges[1:]))


def _ffn_ple(h2, p, w_i, w_o, g1, b1, w_pg, b_pg, w_p, g2, b2, l, *, B, S, D, d_ff, d_ple, ts,
             alpha, batch_major_out):
    n_tiles = B * (S // ts)
    kern = functools.partial(_ffn_kernel, d_ff=d_ff, chunks=_ffn_chunks(d_ff, FFN_CHUNK_COLS),
                             alpha=alpha, n_tiles=n_tiles)
    head, tail = _tile_maps(n_tiles, S // ts)
    if batch_major_out:
        out_spec = pl.BlockSpec((None, ts, D), lambda g: (tail(g)[1], tail(g)[0], 0))
        out_shape = jax.ShapeDtypeStruct((B, S, D), F32)
    else:
        tok_out = pl.BlockSpec((ts, D), lambda g: tail(g))
        out_spec = [tok_out, tok_out]
        out_shape = [jax.ShapeDtypeStruct((S, B * D), F32), jax.ShapeDtypeStruct((S, B * D), BF16)]
    vec = _layer_spec((1, D), l)
    return pl.pallas_call(
        kern,
        grid=(n_tiles + 1,),
        in_specs=[pl.BlockSpec((ts, D), lambda g: head(g)),
                  pl.BlockSpec((None, None, ts, d_ple), lambda g: (l, tail(g)[1], tail(g)[0], 0)),
                  _layer_spec((D, 2 * d_ff), l), _layer_spec((d_ff, D), l), vec, vec,
                  _layer_spec((D, D), l), vec, _layer_spec((d_ple, D), l), vec, vec],
        out_specs=out_spec,
        out_shape=out_shape,
        scratch_shapes=[pltpu.VMEM((ts, D), F32), pltpu.VMEM((ts, D), F32)],
        compiler_params=_params("arbitrary"),
        name="ffn_ple",
    )(h2, p, w_i, w_o, g1, b1, w_pg, b_pg, w_p, g2, b2)


def kernel(x, p, ln_in_g, ln_in_b, w_in, b_forget, conv_w, conv_b, rg_w_a, rg_b_a, rg_w_x, rg_b_x,
           rg_lambda, w_branch_att, w_branch_rnn, b_merge, w_out, ln_mix_g, ln_mix_b, w_ffn_in,
           w_ffn_out, ln_ffn_g, ln_ffn_b, w_ple, w_ple_gate, b_ple_gate, ln_ple_g, ln_ple_b):
    B, S, D = x.shape
    L = w_in.shape[0]
    H = b_forget.shape[1]
    d_att = w_branch_att.shape[1]
    head_dim = d_att // H
    n_blocks, blk = rg_w_a.shape[1], rg_w_a.shape[2]
    d_ff = w_ffn_out.shape[1]
    d_ple = w_ple.shape[1]
    assert head_dim == LANES and blk == LANES and D == n_blocks * blk and d_att == D
    assert H <= KEY_TERM_STRIDE and B % SUBLANES == 0
    alpha = float((2 * L) ** 0.25)

    ts = min(ROW_TILE, S)
    tq = min(ROW_TILE, S)
    tt = max(1, min(S, ROW_TILE // B))

    o_f, o_rx = 3 * d_att, 3 * d_att + H
    o_g = o_rx + 2 * D
    w_qkv = jnp.pad(w_in[:, :, :o_rx], ((0, 0), (0, 0), (0, LANES - H))).astype(BF16)
    b_fc = jnp.pad(b_forget, ((0, 0), (0, LANES - H))).reshape(L, 1, LANES)
    w_r = w_in[:, :, o_rx:o_g].astype(BF16)
    w_g = w_in[:, :, o_g:].astype(BF16)
    w_ax = jnp.concatenate([rg_w_a, rg_w_x], axis=-1).astype(BF16)
    w_ba, w_br, w_o = (w.astype(BF16) for w in (w_branch_att, w_branch_rnn, w_out))
    gate_half = jnp.where(jnp.arange(2 * d_ff) < d_ff, 0.5, 1.0).astype(F32)
    w_fi, w_fo = (w_ffn_in * gate_half).astype(BF16), w_ffn_out.astype(BF16)
    w_pg, w_p = (0.5 * w_ple_gate).astype(BF16), (0.5 * w_ple).astype(BF16)
    vec = lambda a: a.reshape(L, 1, D)

    h2, h16 = _ln_in(x, ln_in_g, ln_in_b, min(2 * ROW_TILE, S))
    for l in range(L):
        qkv, ccol, kf = _qkv_proj(h16, w_qkv, b_fc, l, B=B, S=S, D=D,
                                  n_heads=H, head_dim=head_dim, tm=ts)
        att = _attention(qkv, ccol, kf, B=B, S=S, n_heads=H, head_dim=head_dim, tq=tq, group=H)
        rnn = _rnn_branch(h16, w_r, conv_w, vec(conv_b), w_ax, vec(rg_b_a),
                          vec(rg_b_x), vec(rg_lambda), l, B=B, S=S, D=D, n_blocks=n_blocks, tt=tt)
        h2 = _merge(h2, att, rnn, w_g, w_ba, w_br, w_o, b_merge,
                    vec(ln_mix_g), vec(ln_mix_b), l, B=B, S=S, D=D, n_heads=H, head_dim=head_dim,
                    ts=ts, alpha=alpha)
        out = _ffn_ple(h2, p, w_fi, w_fo, vec(ln_ffn_g), vec(ln_ffn_b), w_pg, vec(b_ple_gate), w_p,
                       vec(ln_ple_g), vec(ln_ple_b), l, B=B, S=S, D=D, d_ff=d_ff, d_ple=d_ple,
                       ts=ts, alpha=alpha, batch_major_out=(l == L - 1))
        if l == L - 1:
            return out
        h2, h16 = out
```
